```python
import jax
import jax.numpy as jnp
from jax import lax
import numpy as np

D_MODEL = 2048
BATCH = 4
SEQ = 2048
DEPTH = 4
DEC_BATCH = 8
DEC_SEQ = 8
PAST_LEN = 16384
PAGE_SIZE = 128

N_MIXERS = 3
N_HEADS = 16
HEAD_DIM = D_MODEL // N_HEADS
KV_HEADS = 4
HPG = N_HEADS // KV_HEADS
KV_WIDTH = KV_HEADS * HEAD_DIM
CMP_BLOCK = 32
CMP_STRIDE = 16
SEL_BLOCK = 64
SEL_TOPK = 16
WINDOW = 512
NSA_QBLOCK = 32
NSA_IN = N_HEADS * HEAD_DIM + 6 * KV_WIDTH + 3 * N_HEADS
ROPE_THETA = 10000.0
FORCE_BONUS = 1e4
D_CONV = D_MODEL
CONV_WIDTH = 3
CHUNK = 128
D_SG = D_MODEL
SG_GROUPS = 8
D_FF = 5632
N_EXPERTS = 8
TOP_K = 2
D_FF_EXPERT = 7168
MOE_BLOCK = 128
LN_EPS = 1e-5
DEEPNORM_ALPHA = (2 * DEPTH) ** 0.25
DEEPNORM_BETA = (8 * DEPTH) ** -0.25
N_NSA = (DEPTH + 2) // 3
N_CONV = (DEPTH + 1) // 3
N_CMLP = DEPTH // 3
N_DENSE = (DEPTH + 1) // 2
N_MOE = DEPTH // 2

kernel_name = 'hybrid_nsa_shortconv_chunkmlp_moe_step'


def layer_norm(x, g, b):
    xf = x.astype(jnp.float32)
    mu = xf.mean(-1, keepdims=True)
    var = jnp.square(xf - mu).mean(-1, keepdims=True)
    return ((xf - mu) * lax.rsqrt(var + LN_EPS) * g + b).astype(x.dtype)


def rope(x, pos):
    half = HEAD_DIM // 2
    inv = ROPE_THETA ** (-jnp.arange(half, dtype=jnp.float32) / half)
    ang = pos.astype(jnp.float32)[:, None] * inv[None, :]
    cos = jnp.cos(ang)[:, None, :]
    sin = jnp.sin(ang)[:, None, :]
    x1 = x[..., :half].astype(jnp.float32)
    x2 = x[..., half:].astype(jnp.float32)
    return jnp.concatenate([x1 * cos - x2 * sin, x2 * cos + x1 * sin], -1).astype(x.dtype)


def masked_softmax(s, mask):
    s = jnp.where(mask, s.astype(jnp.float32), -jnp.inf)
    m = jnp.max(s, axis=-1, keepdims=True)
    m = jnp.where(jnp.isfinite(m), m, 0.0)
    p = jnp.where(mask, jnp.exp(s - m), 0.0)
    return p / jnp.maximum(p.sum(-1, keepdims=True), 1e-30)


def compress_rows(rows, w1, pe, w2):
    B, L = rows.shape[:2]
    sub = rows.reshape(B, L // CMP_STRIDE, CMP_STRIDE, KV_HEADS, HEAD_DIM)
    h = (jnp.einsum('bnlgd,lde->bnge', sub[:, :-1], w1[:CMP_STRIDE])
         + jnp.einsum('bnlgd,lde->bnge', sub[:, 1:], w1[CMP_STRIDE:])
         + jnp.einsum('ld,lde->e', pe, w1))
    return jnp.einsum('bnge,ef->bngf', jax.nn.gelu(h), w2)


def nsa_project(x, pos, w_in):
    B, T = x.shape[:2]
    qd = N_HEADS * HEAD_DIM
    proj = x @ w_in
    q = rope(proj[..., :qd].reshape(B, T, N_HEADS, HEAD_DIM), pos).reshape(B, T, KV_HEADS, HPG, HEAD_DIM)
    kv = proj[..., qd:qd + 6 * KV_WIDTH].reshape(B, T, 6, KV_HEADS, HEAD_DIM)
    keys = rope(kv[:, :, 0::2].reshape(B, T, 3 * KV_HEADS, HEAD_DIM), pos).reshape(B, T, 3, KV_HEADS, HEAD_DIM)
    vals = kv[:, :, 1::2]
    rows = jnp.stack([keys[:, :, 0], vals[:, :, 0], keys[:, :, 1], vals[:, :, 1]], 2)
    win = jnp.stack([keys[:, :, 2], vals[:, :, 2]], 2)
    gates = jax.nn.sigmoid(proj[..., qd + 6 * KV_WIDTH:].astype(jnp.float32))
    gates = gates.reshape(B, T, 3, KV_HEADS, HPG).astype(x.dtype)
    return q, rows, win, gates


def nsa_keys(rows, cmp_w1, cmp_pe, cmp_w2):
    B, L = rows.shape[:2]
    kc = compress_rows(rows[:, :, 0], cmp_w1[0], cmp_pe[0], cmp_w2[0])
    vc = compress_rows(rows[:, :, 1], cmp_w1[1], cmp_pe[1], cmp_w2[1])
    sel = rows[:, :, 2:].reshape(B, L // SEL_BLOCK, SEL_BLOCK, 2, KV_HEADS, HEAD_DIM)
    sel = sel.transpose(3, 0, 4, 1, 2, 5)
    return kc, vc, sel[0], sel[1]


def nsa_attend(q, t_pos, kc, vc, ks_g, vs_g, kw, vw, band_start):
    B = q.shape[0]
    scale = HEAD_DIM ** -0.5
    NC = kc.shape[1]
    NS = ks_g.shape[2]
    s = jnp.einsum('btghd,bngd->btghn', q, kc) * scale
    c_end = jnp.arange(NC, dtype=jnp.int32) * CMP_STRIDE + (CMP_BLOCK - 1)
    cmask = (c_end[None, :] <= t_pos[:, None])[None, :, None, None, :]
    p_cmp = masked_softmax(s, cmask)
    o_cmp = jnp.einsum('btghn,bngd->btghd', p_cmp.astype(vc.dtype), vc)
    pg = p_cmp.sum(3)
    s_sub = jnp.pad(pg, ((0, 0), (0, 0), (0, 0), (1, 0))) + jnp.pad(pg, ((0, 0), (0, 0), (0, 0), (0, 1)))
    p_slc = s_sub.reshape(*pg.shape[:3], NS, SEL_BLOCK // CMP_STRIDE).sum(-1)
    blk = jnp.arange(NS, dtype=jnp.int32)
    cur = t_pos // SEL_BLOCK
    avail = blk[None, :] <= cur[:, None]
    forced = (blk[None, :] == 0) | (blk[None, :] == cur[:, None]) | (blk[None, :] == cur[:, None] - 1)
    score = jnp.where(avail[None, :, None, :],
                      p_slc + jnp.where(forced, FORCE_BONUS, 0.0)[None, :, None, :], -jnp.inf)
    top_s, idx = lax.top_k(score, min(SEL_TOPK, NS))
    sel_valid = jnp.isfinite(top_s)
    bi = jnp.arange(B)[:, None, None, None]
    gi = jnp.arange(KV_HEADS)[None, None, :, None]
    ksel = ks_g[bi, gi, idx]
    vsel = vs_g[bi, gi, idx]
    s = jnp.einsum('btghd,btgkld->btghkl', q, ksel) * scale
    tok_pos = idx[..., None] * SEL_BLOCK + jnp.arange(SEL_BLOCK, dtype=jnp.int32)
    smask = sel_valid[..., None] & (tok_pos <= t_pos[None, :, None, None, None])
    Bq, Tq, G, H = s.shape[:4]
    p_sel = masked_softmax(s.reshape(Bq, Tq, G, H, -1), smask.reshape(Bq, Tq, G, 1, -1))
    o_sel = jnp.einsum('btghn,btgnd->btghd', p_sel.astype(vsel.dtype), vsel.reshape(Bq, Tq, G, -1, HEAD_DIM))
    kpos = band_start + jnp.arange(kw.shape[1], dtype=jnp.int32)
    wmask = ((kpos[None, :] <= t_pos[:, None]) & (kpos[None, :] > t_pos[:, None] - WINDOW)
             & (kpos[None, :] >= 0))
    s = jnp.einsum('btghd,bsgd->btghs', q, kw) * scale
    p_win = masked_softmax(s, wmask[None, :, None, None, :])
    o_win = jnp.einsum('btghs,bsgd->btghd', p_win.astype(vw.dtype), vw)
    return o_cmp, o_sel, o_win


def nsa_output(o_cmp, o_sel, o_win, gates, w_out):
    B, T = o_cmp.shape[:2]
    o = (gates[:, :, 0, :, :, None] * o_cmp + gates[:, :, 1, :, :, None] * o_sel
         + gates[:, :, 2, :, :, None] * o_win)
    return o.reshape(B, T, N_HEADS * HEAD_DIM) @ w_out


def nsa_prompt(x, w_in, cmp_w1, cmp_pe, cmp_w2, w_out):
    B, T = x.shape[:2]
    pos = jnp.arange(T, dtype=jnp.int32)
    q, rows, win, gates = nsa_project(x, pos, w_in)
    kc, vc, ks_g, vs_g = nsa_keys(rows, cmp_w1, cmp_pe, cmp_w2)
    win_pad = jnp.pad(win, ((0, 0), (WINDOW, 0), (0, 0), (0, 0), (0, 0)))
    nblk = T // NSA_QBLOCK
    q_blocks = q.reshape(B, nblk, NSA_QBLOCK, KV_HEADS, HPG, HEAD_DIM).swapaxes(0, 1)

    def one_block(args):
        i, q_b = args
        t0 = i * NSA_QBLOCK
        band = lax.dynamic_slice_in_dim(win_pad, t0, WINDOW + NSA_QBLOCK, axis=1)
        t_pos = t0 + jnp.arange(NSA_QBLOCK, dtype=jnp.int32)
        return nsa_attend(q_b, t_pos, kc, vc, ks_g, vs_g, band[:, :, 0], band[:, :, 1], t0 - WINDOW)

    o_c, o_s, o_w = lax.map(one_block, (jnp.arange(nblk, dtype=jnp.int32), q_blocks))
    unblock = lambda o: o.swapaxes(0, 1).reshape(B, T, KV_HEADS, HPG, HEAD_DIM)
    y = nsa_output(unblock(o_c), unblock(o_s), unblock(o_w), gates, w_out)
    return y, rows, win[:, T - min(WINDOW, T):]


def nsa_sample(x, cache_nsa_kv, m, page_table, win_buf, w_in, cmp_w1, cmp_pe, cmp_w2, w_out):
    B, T = x.shape[:2]
    past_len = page_table.shape[1] * PAGE_SIZE
    pos = past_len + jnp.arange(T, dtype=jnp.int32)
    q, rows, win, gates = nsa_project(x, pos, w_in)
    past = cache_nsa_kv[m, page_table].reshape(B, past_len, 4, KV_HEADS, HEAD_DIM)
    L = past_len + T
    L_pad = -(-L // SEL_BLOCK) * SEL_BLOCK
    all_rows = jnp.concatenate(
        [past, rows, jnp.zeros((B, L_pad - L, 4, KV_HEADS, HEAD_DIM), rows.dtype)], 1)
    kc, vc, ks_g, vs_g = nsa_keys(all_rows, cmp_w1, cmp_pe, cmp_w2)
    nbuf = win_buf.shape[1]
    band = jnp.concatenate([win_buf, win], 1)
    o_c, o_s, o_w = nsa_attend(q, pos, kc, vc, ks_g, vs_g, band[:, :, 0], band[:, :, 1], past_len - nbuf)
    y = nsa_output(o_c, o_s, o_w, gates, w_out)
    return y, rows, band[:, band.shape[1] - nbuf:]


def conv_mixer(x, prev, w_in, w_conv, w_out):
    T = x.shape[1]
    b_gate, c_gate, h = jnp.split(x @ w_in, 3, axis=-1)
    u_ext = jnp.concatenate([prev, c_gate * h], 1)
    conv = sum(w_conv[k] * u_ext[:, k:k + T] for k in range(CONV_WIDTH))
    y = (b_gate * conv) @ w_out
    return y, u_ext[:, u_ext.shape[1] - (CONV_WIDTH - 1):]


def chunk_mlp(x, w_in, ln_g, ln_b, w_s, b_s, w_out):
    B, T = x.shape[:2]
    u, v = jnp.split(jax.nn.gelu(x @ w_in), 2, axis=-1)
    v = layer_norm(v, ln_g, ln_b)
    c = min(T, CHUNK)
    nch = T // c
    ws = jnp.where(jnp.tril(jnp.ones((c, c), bool)), w_s[:, :c, :c], 0.0)
    vg = v.reshape(B, nch, c, SG_GROUPS, D_SG // SG_GROUPS)
    mixed = jnp.einsum('gij,bnjgd->bnigd', ws, vg) + b_s[:, :c].T[None, None, :, :, None]
    y = (u * mixed.reshape(B, T, D_SG)) @ w_out
    return y, v[:, T - c:]


def swiglu(x, w_gu, w_down):
    g, u = jnp.split(x @ w_gu, 2, axis=-1)
    return (jax.nn.silu(g) * u) @ w_down


def moe_ffn(x, router, w_gu, w_down):
    B, T, D = x.shape
    n = B * T
    xt = x.reshape(n, D)
    logits = (xt @ router).astype(jnp.float32)
    top_l, top_e = lax.top_k(logits, TOP_K)
    gate = jax.nn.softmax(top_l, axis=-1).astype(x.dtype)
    A = n * TOP_K
    flat_e = top_e.reshape(A)
    flat_tok = jnp.arange(A, dtype=jnp.int32) // TOP_K
    flat_gate = gate.reshape(A)
    order = jnp.argsort(flat_e)
    sorted_e = flat_e[order]
    counts = jnp.bincount(flat_e, length=N_EXPERTS)
    padded = (counts + MOE_BLOCK - 1) // MOE_BLOCK * MOE_BLOCK
    start = jnp.cumsum(counts) - counts
    pend = jnp.cumsum(padded)
    pstart = pend - padded
    dest = pstart[sorted_e] + jnp.arange(A, dtype=jnp.int32) - start[sorted_e]
    n_blocks = -(-(A + N_EXPERTS * (MOE_BLOCK - 1)) // MOE_BLOCK)
    P = n_blocks * MOE_BLOCK
    row_tok = jnp.full((P,), n, jnp.int32).at[dest].set(flat_tok[order])
    row_gate = jnp.zeros((P,), x.dtype).at[dest].set(flat_gate[order])
    block_e = jnp.minimum(jnp.searchsorted(pend, jnp.arange(n_blocks) * MOE_BLOCK, side='right'),
                          N_EXPERTS - 1)
    xs = jnp.concatenate([xt, jnp.zeros((1, D), x.dtype)], 0)[row_tok].reshape(n_blocks, MOE_BLOCK, D)

    def expert_block(args):
        xb, e = args
        return swiglu(xb, w_gu[e], w_down[e])

    yb = lax.map(expert_block, (xs, block_e)).reshape(P, D)
    out = jnp.zeros((n + 1, D), x.dtype).at[row_tok].add(yb * row_gate[:, None])
    return out[:n].reshape(B, T, D)


def setup_inputs(seed: int = 0) -> dict:
    key = jax.random.key(seed)
    k = jax.random.split(key, 32)
    n_pages = PAST_LEN // PAGE_SIZE
    n_used = DEC_BATCH * n_pages
    n_pool = n_used + n_used // 4
    win_buf = min(WINDOW, PAST_LEN)

    def nrm(kk, shape, scale=1.0):
        return jax.random.normal(kk, shape, jnp.float32) * scale

    page_table = jax.random.permutation(k[5], n_pool)[:n_used].reshape(DEC_BATCH, n_pages).astype(jnp.int32)
    return {
        'x_prompt': nrm(k[0], (BATCH, SEQ, D_MODEL)),
        'x_sample': nrm(k[1], (DEC_BATCH, DEC_SEQ, D_MODEL)),
        'cache_nsa_kv': nrm(k[2], (N_NSA, n_pool, PAGE_SIZE, 4, KV_HEADS, HEAD_DIM)),
        'state_nsa_win': nrm(k[3], (N_NSA, DEC_BATCH, win_buf, 2, KV_HEADS, HEAD_DIM)),
        'state_conv': nrm(k[4], (N_CONV, DEC_BATCH, CONV_WIDTH - 1, D_CONV)),
        'page_table': page_table,
        'ln_gain': 1.0 + nrm(k[6], (DEPTH, 2, D_MODEL), 0.05),
        'ln_bias': nrm(k[7], (DEPTH, 2, D_MODEL), 0.02),
        'nsa_w_in': nrm(k[8], (N_NSA, D_MODEL, NSA_IN), D_MODEL ** -0.5),
        'nsa_cmp_w1': nrm(k[9], (N_NSA, 2, CMP_BLOCK, HEAD_DIM, HEAD_DIM), (CMP_BLOCK * HEAD_DIM) ** -0.5),
        'nsa_cmp_pe': nrm(k[10], (N_NSA, 2, CMP_BLOCK, HEAD_DIM), 0.5),
        'nsa_cmp_w2': nrm(k[11], (N_NSA, 2, HEAD_DIM, HEAD_DIM), HEAD_DIM ** -0.5),
        'nsa_w_out': nrm(k[12], (N_NSA, N_HEADS * HEAD_DIM, D_MODEL), DEEPNORM_BETA * (N_HEADS * HEAD_DIM) ** -0.5),
        'conv_w_in': nrm(k[13], (N_CONV, D_MODEL, 3 * D_CONV), D_MODEL ** -0.5),
        'conv_w': nrm(k[14], (N_CONV, CONV_WIDTH, D_CONV), CONV_WIDTH ** -0.5),
        'conv_w_out': nrm(k[15], (N_CONV, D_CONV, D_MODEL), DEEPNORM_BETA * D_CONV ** -0.5),
        'cmlp_w_in': nrm(k[16], (N_CMLP, D_MODEL, 2 * D_SG), D_MODEL ** -0.5),
        'cmlp_ln_gain': 1.0 + nrm(k[17], (N_CMLP, D_SG), 0.05),
        'cmlp_ln_bias': nrm(k[18], (N_CMLP, D_SG), 0.02),
        'cmlp_w_s': nrm(k[19], (N_CMLP, SG_GROUPS, CHUNK, CHUNK), CHUNK ** -0.5),
        'cmlp_b_s': 1.0 + nrm(k[20], (N_CMLP, SG_GROUPS, CHUNK), 0.05),
        'cmlp_w_out': nrm(k[21], (N_CMLP, D_SG, D_MODEL), DEEPNORM_BETA * D_SG ** -0.5),
        'ffn_w_gu': nrm(k[22], (N_DENSE, D_MODEL, 2 * D_FF), D_MODEL ** -0.5),
        'ffn_w_down': nrm(k[23], (N_DENSE, D_FF, D_MODEL), DEEPNORM_BETA * D_FF ** -0.5),
        'moe_router': nrm(k[24], (N_MOE, D_MODEL, N_EXPERTS), D_MODEL ** -0.5),
        'moe_w_gu': nrm(k[25], (N_MOE, N_EXPERTS, D_MODEL, 2 * D_FF_EXPERT), D_MODEL ** -0.5),
        'moe_w_down': nrm(k[26], (N_MOE, N_EXPERTS, D_FF_EXPERT, D_MODEL), DEEPNORM_BETA * D_FF_EXPERT ** -0.5),
    }


def reference(x_prompt, x_sample, cache_nsa_kv, state_nsa_win, state_conv, page_table,
              ln_gain, ln_bias, nsa_w_in, nsa_cmp_w1, nsa_cmp_pe, nsa_cmp_w2, nsa_w_out,
              conv_w_in, conv_w, conv_w_out, cmlp_w_in, cmlp_ln_gain, cmlp_ln_bias, cmlp_w_s,
              cmlp_b_s, cmlp_w_out, ffn_w_gu, ffn_w_down, moe_router, moe_w_gu, moe_w_down):
    xp, xs = x_prompt, x_sample
    rows_p, rows_s, win_p, win_s = [], [], [], []
    conv_p, conv_s, v_p, v_s = [], [], [], []
    for i in range(DEPTH):
        m = i // N_MIXERS
        kind = i % N_MIXERS
        if kind == 0:
            w = (nsa_w_in[m], nsa_cmp_w1[m], nsa_cmp_pe[m], nsa_cmp_w2[m], nsa_w_out[m])
            hp, rp, wp = nsa_prompt(xp, *w)
            hs, rs, wsn = nsa_sample(xs, cache_nsa_kv, m, page_table, state_nsa_win[m], *w)
            rows_p.append(rp)
            rows_s.append(rs)
            win_p.append(wp)
            win_s.append(wsn)
        elif kind == 1:
            zeros_prev = jnp.zeros((xp.shape[0], CONV_WIDTH - 1, D_CONV), xp.dtype)
            hp, cp = conv_mixer(xp, zeros_prev, conv_w_in[m], conv_w[m], conv_w_out[m])
            hs, cs = conv_mixer(xs, state_conv[m], conv_w_in[m], conv_w[m], conv_w_out[m])
            conv_p.append(cp)
            conv_s.append(cs)
        else:
            w = (cmlp_w_in[m], cmlp_ln_gain[m], cmlp_ln_bias[m], cmlp_w_s[m], cmlp_b_s[m], cmlp_w_out[m])
            hp, vp = chunk_mlp(xp, *w)
            hs, vs = chunk_mlp(xs, *w)
            v_p.append(vp)
            v_s.append(vs)
        xp = layer_norm(DEEPNORM_ALPHA * xp + hp, ln_gain[i, 0], ln_bias[i, 0])
        xs = layer_norm(DEEPNORM_ALPHA * xs + hs, ln_gain[i, 0], ln_bias[i, 0])
        f = i // 2
        if i % 2 == 0:
            fp = swiglu(xp, ffn_w_gu[f], ffn_w_down[f])
            fs = swiglu(xs, ffn_w_gu[f], ffn_w_down[f])
        else:
            fp = moe_ffn(xp, moe_router[f], moe_w_gu[f], moe_w_down[f])
            fs = moe_ffn(xs, moe_router[f], moe_w_gu[f], moe_w_down[f])
        xp = layer_norm(DEEPNORM_ALPHA * xp + fp, ln_gain[i, 1], ln_bias[i, 1])
        xs = layer_norm(DEEPNORM_ALPHA * xs + fs, ln_gain[i, 1], ln_bias[i, 1])
    p_nsa_rows = jnp.stack(rows_p)
    s_nsa_rows = jnp.stack(rows_s)
    p_nsa_win = jnp.stack(win_p)
    s_nsa_win = jnp.stack(win_s)
    p_conv = jnp.stack(conv_p)
    s_conv = jnp.stack(conv_s)
    p_cmlp_v = jnp.stack(v_p)
    s_cmlp_v = jnp.stack(v_s)
    return (xp, xs, p_nsa_rows, s_nsa_rows, p_nsa_win, s_nsa_win, p_conv, s_conv, p_cmlp_v, s_cmlp_v)
```

```python
import functools

import jax
import jax.numpy as jnp
from jax import lax
from jax.experimental import pallas as pl
from jax.experimental.pallas import tpu as pltpu

D_MODEL = 2048
DEPTH = 4
PAGE_SIZE = 128
N_HEADS = 16
HEAD_DIM = 128
KV_HEADS = 4
HPG = N_HEADS // KV_HEADS
KV_WIDTH = KV_HEADS * HEAD_DIM
CMP_BLOCK = 32
CMP_STRIDE = 16
SEL_BLOCK = 64
SEL_TOPK = 16
WINDOW = 512
ROPE_THETA = 10000.0
FORCE_BONUS = 1e4
CONV_WIDTH = 3
CHUNK = 128
SG_GROUPS = 8
N_EXPERTS = 8
TOP_K = 2
LN_EPS = 1e-5
DEEPNORM_ALPHA = (2 * DEPTH) ** 0.25

LANES = 128
PAGES_PER_STEP = 8
NEG = -1e30
VMEM_LIMIT = 56 * 1024 * 1024

F32 = jnp.float32
BF16 = jnp.bfloat16


def _cparams(sem):
    return pltpu.CompilerParams(dimension_semantics=sem, vmem_limit_bytes=VMEM_LIMIT)


def _ln(y, g, b):
    mu = jnp.mean(y, axis=-1, keepdims=True)
    d = y - mu
    var = jnp.mean(d * d, axis=-1, keepdims=True)
    return d * lax.rsqrt(var + LN_EPS) * g + b


def _dot(a, b):
    return jnp.dot(a, b, preferred_element_type=F32)


def _dot_nt(a, b, precision=None):
    return lax.dot_general(a, b, (((1,), (1,)), ((), ())), preferred_element_type=F32,
                           precision=precision)


def _linear_kernel(x_ref, w_ref, o_ref, *, act):
    y = _dot(x_ref[...].astype(BF16), w_ref[...].astype(BF16))
    if act == "gelu":
        y = jax.nn.gelu(y)
    elif act == "sigmoid":
        y = jax.nn.sigmoid(y)
    o_ref[...] = y.astype(o_ref.dtype)


def linear(x, w, *, tm, tn, act=None, out_dtype=F32):
    M, K = x.shape
    N = w.shape[1]
    return pl.pallas_call(
        functools.partial(_linear_kernel, act=act),
        grid=(M // tm, N // tn),
        in_specs=[pl.BlockSpec((tm, K), lambda i, j: (i, 0)),
                  pl.BlockSpec((K, tn), lambda i, j: (0, j))],
        out_specs=pl.BlockSpec((tm, tn), lambda i, j: (i, j)),
        out_shape=jax.ShapeDtypeStruct((M, N), out_dtype),
        compiler_params=_cparams(("parallel", "arbitrary")),
        name="linear",
    )(x, w)


def _swiglu_kernel(x_ref, wg_ref, wu_ref, o_ref):
    x = x_ref[...].astype(BF16)
    g = _dot(x, wg_ref[...].astype(BF16))
    u = _dot(x, wu_ref[...].astype(BF16))
    o_ref[...] = (jax.nn.silu(g) * u).astype(o_ref.dtype)


def linear_swiglu(x, w_gu, *, tm, tn):
    M, K = x.shape
    F = w_gu.shape[1] // 2
    nf = F // tn
    return pl.pallas_call(
        _swiglu_kernel,
        grid=(M // tm, nf),
        in_specs=[pl.BlockSpec((tm, K), lambda i, j: (i, 0)),
                  pl.BlockSpec((K, tn), lambda i, j: (0, j)),
                  pl.BlockSpec((K, tn), lambda i, j: (0, j + nf))],
        out_specs=pl.BlockSpec((tm, tn), lambda i, j: (i, j)),
        out_shape=jax.ShapeDtypeStruct((M, F), BF16),
        compiler_params=_cparams(("parallel", "arbitrary")),
        name="linear_swiglu",
    )(x, w_gu, w_gu)


def _linear_res_ln_kernel(h_ref, w_ref, res_ref, g_ref, b_ref, o_ref, ob_ref, acc_ref):
    k = pl.program_id(1)

    @pl.when(k == 0)
    def _():
        acc_ref[...] = jnp.zeros_like(acc_ref)

    acc_ref[...] += _dot(h_ref[...].astype(BF16), w_ref[...].astype(BF16))

    @pl.when(k == pl.num_programs(1) - 1)
    def _():
        y = _ln(DEEPNORM_ALPHA * res_ref[...] + acc_ref[...], g_ref[...], b_ref[...])
        o_ref[...] = y
        ob_ref[...] = y.astype(BF16)


def linear_res_ln(h, w, res, gain, bias, *, tm, tk):
    M, K = h.shape
    N = w.shape[1]
    return pl.pallas_call(
        _linear_res_ln_kernel,
        grid=(M // tm, K // tk),
        in_specs=[pl.BlockSpec((tm, tk), lambda i, k: (i, k)),
                  pl.BlockSpec((tk, N), lambda i, k: (k, 0)),
                  pl.BlockSpec((tm, N), lambda i, k: (i, 0)),
                  pl.BlockSpec((1, N), lambda i, k: (0, 0)),
                  pl.BlockSpec((1, N), lambda i, k: (0, 0))],
        out_specs=[pl.BlockSpec((tm, N), lambda i, k: (i, 0)),
                   pl.BlockSpec((tm, N), lambda i, k: (i, 0))],
        out_shape=[jax.ShapeDtypeStruct((M, N), F32), jax.ShapeDtypeStruct((M, N), BF16)],
        scratch_shapes=[pltpu.VMEM((tm, N), F32)],
        compiler_params=_cparams(("parallel", "arbitrary")),
        name="linear_res_ln",
    )(h, w, res, gain.reshape(1, N), bias.reshape(1, N))


def _res_ln_kernel(res_ref, f_ref, g_ref, b_ref, o_ref, ob_ref):
    y = _ln(DEEPNORM_ALPHA * res_ref[...] + f_ref[...], g_ref[...], b_ref[...])
    o_ref[...] = y
    ob_ref[...] = y.astype(BF16)


def res_ln(res, f, gain, bias, *, tm):
    M, N = res.shape
    row = pl.BlockSpec((tm, N), lambda i: (i, 0))
    vec = pl.BlockSpec((1, N), lambda i: (0, 0))
    return pl.pallas_call(
        _res_ln_kernel,
        grid=(M // tm,),
        in_specs=[row, row, vec, vec],
        out_specs=[row, row],
        out_shape=[jax.ShapeDtypeStruct((M, N), F32), jax.ShapeDtypeStruct((M, N), BF16)],
        compiler_params=_cparams(("parallel",)),
        name="res_ln",
    )(res, f, gain.reshape(1, N), bias.reshape(1, N))


def _rope_tile(y, cos, sin):
    outs = []
    for c in range(y.shape[1] // HEAD_DIM):
        yc = y[:, c * HEAD_DIM:(c + 1) * HEAD_DIM]
        outs.append(yc * cos + pltpu.roll(yc, HEAD_DIM // 2, axis=1) * sin)
    return jnp.concatenate(outs, axis=1)


def _nsa_proj_kernel(x_ref, w_ref, cos_ref, sin_ref, q_ref, rows_ref, win_ref):
    j = pl.program_id(1)
    y = _dot(x_ref[...].astype(BF16), w_ref[...].astype(BF16))
    cos = cos_ref[...]
    sin = sin_ref[...]

    @pl.when(j < 4)
    def _():
        q_ref[...] = (_rope_tile(y, cos, sin) * (HEAD_DIM ** -0.5)).astype(q_ref.dtype)

    @pl.when((j == 4) | (j == 6))
    def _():
        rows_ref[...] = _rope_tile(y, cos, sin)

    @pl.when((j == 5) | (j == 7))
    def _():
        rows_ref[...] = y

    @pl.when(j == 8)
    def _():
        win_ref[...] = _rope_tile(y, cos, sin)

    @pl.when(j == 9)
    def _():
        win_ref[...] = y


def nsa_proj(x, w_in, cos, sin, *, tm, q_dtype):
    M, K = x.shape
    tn = KV_WIDTH
    qd = N_HEADS * HEAD_DIM
    return pl.pallas_call(
        _nsa_proj_kernel,
        grid=(M // tm, 10),
        in_specs=[pl.BlockSpec((tm, K), lambda i, j: (i, 0)),
                  pl.BlockSpec((K, tn), lambda i, j: (0, j)),
                  pl.BlockSpec((tm, HEAD_DIM), lambda i, j: (i, 0)),
                  pl.BlockSpec((tm, HEAD_DIM), lambda i, j: (i, 0))],
        out_specs=[pl.BlockSpec((tm, tn), lambda i, j: (i, jnp.minimum(j, 3))),
                   pl.BlockSpec((tm, tn), lambda i, j: (i, jnp.clip(j - 4, 0, 3))),
                   pl.BlockSpec((tm, tn), lambda i, j: (i, jnp.clip(j - 8, 0, 1)))],
        out_shape=[jax.ShapeDtypeStruct((M, qd), q_dtype),
                   jax.ShapeDtypeStruct((M, 4 * KV_WIDTH), F32),
                   jax.ShapeDtypeStruct((M, 2 * KV_WIDTH), F32)],
        compiler_params=_cparams(("parallel", "arbitrary")),
        name="nsa_proj",
    )(x, w_in, cos, sin)


def rope_tables(pos):
    half = HEAD_DIM // 2
    inv = ROPE_THETA ** (-jnp.arange(half, dtype=F32) / half)
    ang = pos.astype(F32)[:, None] * inv[None, :]
    cos = jnp.cos(ang)
    sin = jnp.sin(ang)
    return jnp.concatenate([cos, cos], -1), jnp.concatenate([-sin, sin], -1)


def _compress_kernel(pt_ref, *refs):
    npg = PAGES_PER_STEP
    page_refs = refs[:npg]
    wcat_ref, pe_ref, w2_ref, o_ref, carry_ref = refs[npg:]
    j = pl.program_id(1)
    nch = 2 * KV_HEADS
    nsub = PAGE_SIZE // CMP_STRIDE
    R = npg * nsub * nch

    @pl.when(j == 0)
    def _():
        carry_ref[...] = jnp.zeros_like(carry_ref)

    acc = jnp.zeros((R, 4 * HEAD_DIM), F32)
    pe_terms = [jnp.zeros((1, HEAD_DIM), F32), jnp.zeros((1, HEAD_DIM), F32)]
    for l in range(CMP_STRIDE):
        x = jnp.concatenate([page_refs[p][0, :, l].reshape(nsub * nch, HEAD_DIM) for p in range(npg)], axis=0)
        w = wcat_ref[l]
        acc = acc + _dot(x.astype(BF16), w.astype(BF16))
        for t in range(2):
            d = jnp.dot(pe_ref[t], w[:, 2 * t * HEAD_DIM:2 * (t + 1) * HEAD_DIM],
                        preferred_element_type=F32, precision=lax.Precision.HIGHEST)
            pe_terms[t] = (pe_terms[t] + d[l:l + 1, :HEAD_DIM]
                           + d[CMP_STRIDE + l:CMP_STRIDE + l + 1, HEAD_DIM:])
    outs = []
    for t in range(2):
        a = acc[:, 2 * t * HEAD_DIM:(2 * t + 1) * HEAD_DIM]
        b = acc[:, (2 * t + 1) * HEAD_DIM:2 * (t + 1) * HEAD_DIM]
        a_shift = jnp.concatenate([carry_ref[t], a[:R - nch]], axis=0)
        carry_ref[t] = a[R - nch:]
        h = a_shift + b + pe_terms[t]
        outs.append(_dot(jax.nn.gelu(h).astype(BF16), w2_ref[t].astype(BF16)))
    is_k = lax.broadcasted_iota(jnp.int32, (R, HEAD_DIM), 0) % nch < KV_HEADS
    o_ref[0] = jnp.where(is_k, outs[0], outs[1])


def compress(pages, page_ids, n_batch, wcat, pe, w2):
    npg = PAGES_PER_STEP
    nch = 2 * KV_HEADS
    nsub = PAGE_SIZE // CMP_STRIDE
    ppb = page_ids.shape[0] // n_batch
    steps = ppb // npg
    R = npg * nsub * nch

    def page_spec(p):
        return pl.BlockSpec((1, nsub, CMP_STRIDE, nch, HEAD_DIM),
                            lambda b, j, pt: (pt[b * ppb + j * npg + p], 0, 0, 0, 0))

    const = lambda shape: pl.BlockSpec(shape, lambda b, j, pt: (0,) * len(shape))
    out = pl.pallas_call(
        _compress_kernel,
        grid_spec=pltpu.PrefetchScalarGridSpec(
            num_scalar_prefetch=1,
            grid=(n_batch, steps),
            in_specs=[page_spec(p) for p in range(npg)]
            + [const(wcat.shape), const(pe.shape), const(w2.shape)],
            out_specs=pl.BlockSpec((1, R, HEAD_DIM), lambda b, j, pt: (b, j, 0)),
            scratch_shapes=[pltpu.VMEM((2, nch, HEAD_DIM), F32)]),
        out_shape=jax.ShapeDtypeStruct((n_batch, steps * R, HEAD_DIM), F32),
        compiler_params=_cparams(("parallel", "arbitrary")),
        name="nsa_compress",
    )(page_ids, *([pages] * npg), wcat, pe, w2)
    return out.reshape(n_batch, steps * npg * nsub, nch * HEAD_DIM)


def _softmax_rows(s, mask):
    sm = jnp.where(mask, s, NEG)
    m = jnp.max(sm, axis=-1, keepdims=True)
    p = jnp.where(mask, jnp.exp(sm - m), 0.0)
    return p * (1.0 / jnp.maximum(jnp.sum(p, axis=-1, keepdims=True), 1e-30))


def _online_step(s, mask, v, m, l, acc):
    sm = jnp.where(mask, s, NEG)
    m_new = jnp.maximum(m, jnp.max(sm, axis=-1, keepdims=True))
    p = jnp.where(mask, jnp.exp(sm - m_new), 0.0)
    alpha = jnp.exp(m - m_new)
    l = alpha * l + jnp.sum(p, axis=-1, keepdims=True)
    acc = alpha * acc + _dot(p.astype(BF16), v)
    return m_new, l, acc


def _pool_matrix(n_rows, n_cols, transposed):
    shape = (n_cols, n_rows) if transposed else (n_rows, n_cols)
    n_ax, j_ax = (1, 0) if transposed else (0, 1)
    d = lax.broadcasted_iota(jnp.int32, shape, n_ax) - 4 * lax.broadcasted_iota(jnp.int32, shape, j_ax)
    return jnp.where((d == 0) | (d == 4), 1.0, jnp.where((d > 0) & (d < 4), 2.0, 0.0)).astype(F32)


TQ = 128
SEL_CHUNK = 512


def _nsa_prompt_kernel(q_ref, kc_ref, vc_ref, ks_ref, vs_ref, kw_ref, vw_ref, gate_ref, o_ref):
    qi = pl.program_id(2)
    t0 = qi * TQ
    R = HPG * TQ
    q = q_ref[...]
    qs = jnp.concatenate([q[:, h * HEAD_DIM:(h + 1) * HEAD_DIM] for h in range(HPG)], axis=0)
    t_row = t0 + lax.broadcasted_iota(jnp.int32, (TQ, 1), 0)
    t_rows = jnp.concatenate([t_row] * HPG, axis=0)

    nc = kc_ref.shape[1]
    s = _dot_nt(qs, kc_ref[0].astype(BF16))
    n_idx = lax.broadcasted_iota(jnp.int32, (1, nc), 1)
    cmask = (n_idx >= 1) & (CMP_STRIDE * n_idx + (CMP_STRIDE - 1) <= t_rows)
    p_cmp = _softmax_rows(s, cmask)
    o_cmp = _dot(p_cmp.astype(BF16), vc_ref[0].astype(BF16))
    pg = p_cmp[0:TQ]
    for h in range(1, HPG):
        pg = pg + p_cmp[h * TQ:(h + 1) * TQ]

    ns = ks_ref.shape[1] // SEL_BLOCK
    p_slc = _dot_nt(_pool_matrix(nc, ns, True), pg, precision=lax.Precision.HIGHEST)
    blk = lax.broadcasted_iota(jnp.int32, (ns, TQ), 0)
    cur = (t0 + lax.broadcasted_iota(jnp.int32, (ns, TQ), 1)) // SEL_BLOCK
    avail = blk <= cur
    forced = (blk == 0) | (blk == cur) | (blk == cur - 1)
    score = jnp.where(avail, p_slc + jnp.where(forced, FORCE_BONUS, 0.0), -jnp.inf)
    rank = jnp.zeros((ns, TQ), jnp.int32)
    for jj in range(ns):
        row = score[jj:jj + 1, :]
        beats = (row > score) | ((row == score) & (blk > jj))
        rank = rank + beats.astype(jnp.int32)
    sel_t = jnp.where((rank < SEL_TOPK) & avail, 1.0, 0.0).astype(F32)
    sel_t = jnp.concatenate([sel_t, jnp.zeros((LANES - ns, TQ), F32)], axis=0)
    sel = sel_t.T.astype(BF16)

    def sel_body(c, carry):
        k0 = pl.multiple_of(c * SEL_CHUNK, SEL_CHUNK)
        key = k0 + lax.broadcasted_iota(jnp.int32, (LANES, SEL_CHUNK), 1)
        expand = (lax.broadcasted_iota(jnp.int32, (LANES, SEL_CHUNK), 0) == key // SEL_BLOCK)
        picked = _dot(sel, expand.astype(BF16)) > 0.5
        kpos = k0 + lax.broadcasted_iota(jnp.int32, (1, SEL_CHUNK), 1)
        m1 = picked & (kpos <= t_row)
        mask = jnp.concatenate([m1] * HPG, axis=0)
        k = ks_ref[0, pl.ds(k0, SEL_CHUNK), :].astype(BF16)
        v = vs_ref[0, pl.ds(k0, SEL_CHUNK), :].astype(BF16)
        return _online_step(_dot_nt(qs, k), mask, v, *carry)

    init = (jnp.full((R, 1), NEG, F32), jnp.zeros((R, 1), F32), jnp.zeros((R, HEAD_DIM), F32))
    n_chunks = (t0 + TQ + SEL_CHUNK - 1) // SEL_CHUNK
    _, l, acc = lax.fori_loop(0, n_chunks, sel_body, init)
    o_sel = acc * (1.0 / jnp.maximum(l, 1e-30))

    band = WINDOW + TQ
    start = pl.multiple_of(jnp.maximum(t0 - WINDOW, 0), TQ)
    kpos = start + lax.broadcasted_iota(jnp.int32, (1, band), 1)
    wmask = (kpos <= t_rows) & (kpos > t_rows - WINDOW)
    s = _dot_nt(qs, kw_ref[0, pl.ds(start, band), :].astype(BF16))
    p_win = _softmax_rows(s, wmask)
    o_win = _dot(p_win.astype(BF16), vw_ref[0, pl.ds(start, band), :].astype(BF16))

    gates = gate_ref[0]
    for h in range(HPG):
        rows = slice(h * TQ, (h + 1) * TQ)
        o = (gates[:, h:h + 1] * o_cmp[rows] + gates[:, HPG + h:HPG + h + 1] * o_sel[rows]
             + gates[:, 2 * HPG + h:2 * HPG + h + 1] * o_win[rows])
        o_ref[:, h * HEAD_DIM:(h + 1) * HEAD_DIM] = o.astype(o_ref.dtype)


def nsa_prompt_attend(q, cmp, rows, win, gates_g, B, T):
    nq = T // TQ
    kv = lambda col0: pl.BlockSpec((1, T, HEAD_DIM), lambda b, g, i: (b, 0, col0 + g))
    cmp_spec = lambda col0: pl.BlockSpec((1, cmp.shape[1], HEAD_DIM), lambda b, g, i: (b, 0, col0 + g))
    return pl.pallas_call(
        _nsa_prompt_kernel,
        grid=(B, KV_HEADS, nq),
        in_specs=[pl.BlockSpec((TQ, HPG * HEAD_DIM), lambda b, g, i: (b * nq + i, g)),
                  cmp_spec(0), cmp_spec(KV_HEADS),
                  kv(2 * KV_HEADS), kv(3 * KV_HEADS), kv(0), kv(KV_HEADS),
                  pl.BlockSpec((1, TQ, LANES), lambda b, g, i: (g, b * nq + i, 0))],
        out_specs=pl.BlockSpec((TQ, HPG * HEAD_DIM), lambda b, g, i: (b * nq + i, g)),
        out_shape=jax.ShapeDtypeStruct((B * T, N_HEADS * HEAD_DIM), BF16),
        compiler_params=_cparams(("parallel", "parallel", "arbitrary")),
        name="nsa_prompt_attend",
    )(q, cmp, cmp, rows, rows, win, win, gates_g)


def _nsa_sample_select_kernel(q_ref, cmp_ref, ocmp_ref, sel_ref, *, past_len, ns_pad):
    T = q_ref.shape[1]
    R = HPG * T
    nc = cmp_ref.shape[1]
    q = q_ref[0]
    t_pos = past_len + lax.broadcasted_iota(jnp.int32, (T, 1), 0)
    t_rows = jnp.concatenate([t_pos] * HPG, axis=0)
    n_idx = lax.broadcasted_iota(jnp.int32, (1, nc), 1)
    cmask = (n_idx >= 1) & (CMP_STRIDE * n_idx + (CMP_STRIDE - 1) <= t_rows)
    pool = _pool_matrix(nc, ns_pad, False)
    pgs = []
    for g in range(KV_HEADS):
        qs = jnp.concatenate(
            [q[:, (g * HPG + h) * HEAD_DIM:(g * HPG + h + 1) * HEAD_DIM] for h in range(HPG)],
            axis=0).astype(BF16)
        kc = cmp_ref[0, :, g * HEAD_DIM:(g + 1) * HEAD_DIM].astype(BF16)
        vc = cmp_ref[0, :, (KV_HEADS + g) * HEAD_DIM:(KV_HEADS + g + 1) * HEAD_DIM].astype(BF16)
        p = _softmax_rows(_dot_nt(qs, kc), cmask)
        ocmp_ref[0, g] = _dot(p.astype(BF16), vc)
        pg = p[0:T]
        for h in range(1, HPG):
            pg = pg + p[h * T:(h + 1) * T]
        pgs.append(pg)
    pg = jnp.concatenate(pgs, axis=0)
    p_slc = jnp.dot(pg, pool, preferred_element_type=F32, precision=lax.Precision.HIGHEST)
    blk = lax.broadcasted_iota(jnp.int32, (KV_HEADS * T, ns_pad), 1)
    cur = jnp.concatenate([t_pos] * KV_HEADS, axis=0) // SEL_BLOCK
    avail = blk <= cur
    forced = (blk == 0) | (blk == cur) | (blk == cur - 1)
    score = jnp.where(avail, p_slc + jnp.where(forced, FORCE_BONUS, 0.0), -jnp.inf)
    sel = jnp.zeros(score.shape, F32)
    blk_f = blk.astype(F32)
    for _ in range(SEL_TOPK):
        m = jnp.max(score, axis=-1, keepdims=True)
        first = jnp.min(jnp.where(score == m, blk_f, float(ns_pad)), axis=-1, keepdims=True)
        pick = blk_f == first
        sel = jnp.where(pick & (m > -jnp.inf), 1.0, sel)
        score = jnp.where(pick, -jnp.inf, score)
    sel_ref[0] = sel


def _nsa_sample_attend_kernel(pt_ref, *refs, past_len, ns_pad):
    npg = PAGES_PER_STEP
    page_refs = refs[:npg]
    (q_ref, sel_ref, new_ref, band_ref, ocmp_ref, gate_ref, o_ref, m_ref, l_ref, acc_ref) = refs[npg:]
    c = pl.program_id(1)
    T = q_ref.shape[1]
    R = HPG * T
    chunk = npg * PAGE_SIZE
    q = q_ref[0]
    t_pos = past_len + lax.broadcasted_iota(jnp.int32, (T, 1), 0)
    t_rows = jnp.concatenate([t_pos] * HPG, axis=0)

    @pl.when(c == 0)
    def _():
        m_ref[...] = jnp.full(m_ref.shape, NEG, F32)
        l_ref[...] = jnp.zeros_like(l_ref)
        acc_ref[...] = jnp.zeros_like(acc_ref)

    def qs_of(g):
        return jnp.concatenate(
            [q[:, (g * HPG + h) * HEAD_DIM:(g * HPG + h + 1) * HEAD_DIM] for h in range(HPG)],
            axis=0).astype(BF16)

    def picked_masks(first_key, n_keys):
        key = first_key + lax.broadcasted_iota(jnp.int32, (ns_pad, n_keys), 1)
        expand = lax.broadcasted_iota(jnp.int32, (ns_pad, n_keys), 0) == key // SEL_BLOCK
        hit = _dot(sel_ref[0].astype(BF16), expand.astype(BF16))
        kpos = first_key + lax.broadcasted_iota(jnp.int32, (1, n_keys), 1)
        out = []
        for g in range(KV_HEADS):
            m1 = (hit[g * T:(g + 1) * T] > 0.5) & (kpos <= t_pos)
            out.append(jnp.concatenate([m1] * HPG, axis=0))
        return out

    def update(g, s, mask, v):
        m, l, acc = _online_step(s, mask, v, m_ref[g][:, 0:1], l_ref[g][:, 0:1], acc_ref[g])
        m_ref[g] = jnp.broadcast_to(m, (R, LANES))
        l_ref[g] = jnp.broadcast_to(l, (R, LANES))
        acc_ref[g] = acc

    masks = picked_masks(c * chunk, chunk)
    for g in range(KV_HEADS):
        qs = qs_of(g)
        k = jnp.concatenate([page_refs[p][0, :, g * HEAD_DIM:(g + 1) * HEAD_DIM] for p in range(npg)],
                            axis=0).astype(BF16)
        v = jnp.concatenate([page_refs[p][0, :, KV_WIDTH + g * HEAD_DIM:KV_WIDTH + (g + 1) * HEAD_DIM]
                             for p in range(npg)], axis=0).astype(BF16)
        update(g, _dot_nt(qs, k), masks[g], v)

    @pl.when(c == pl.num_programs(1) - 1)
    def _():
        n_new = new_ref.shape[1]
        n_band = band_ref.shape[1]
        gates = gate_ref[0]
        new_masks = picked_masks(past_len, n_new)
        for g in range(KV_HEADS):
            qs = qs_of(g)
            k = new_ref[0, :, g * HEAD_DIM:(g + 1) * HEAD_DIM].astype(BF16)
            v = new_ref[0, :, KV_WIDTH + g * HEAD_DIM:KV_WIDTH + (g + 1) * HEAD_DIM].astype(BF16)
            update(g, _dot_nt(qs, k), new_masks[g], v)
            o_sel = acc_ref[g] * (1.0 / jnp.maximum(l_ref[g][:, 0:1], 1e-30))
            kpos = (past_len - WINDOW) + lax.broadcasted_iota(jnp.int32, (1, n_band), 1)
            wmask = (kpos <= t_rows) & (kpos > t_rows - WINDOW)
            kw = band_ref[0, :, g * HEAD_DIM:(g + 1) * HEAD_DIM].astype(BF16)
            vw = band_ref[0, :, KV_WIDTH + g * HEAD_DIM:KV_WIDTH + (g + 1) * HEAD_DIM].astype(BF16)
            p_win = _softmax_rows(_dot_nt(qs, kw), wmask)
            o_win = _dot(p_win.astype(BF16), vw)
            o_cmp = ocmp_ref[0, g]
            for h in range(HPG):
                rows = slice(h * T, (h + 1) * T)
                col = g * HPG + h
                o = (gates[:, col:col + 1] * o_cmp[rows]
                     + gates[:, N_HEADS + col:N_HEADS + col + 1] * o_sel[rows]
                     + gates[:, 2 * N_HEADS + col:2 * N_HEADS + col + 1] * o_win[rows])
                o_ref[0, :, col * HEAD_DIM:(col + 1) * HEAD_DIM] = o


def nsa_sample_attend(q, cmp, pages, page_ids, new_rows, band, gates, past_len):
    B, T, _ = q.shape
    npg = PAGES_PER_STEP
    ppb = page_ids.shape[0] // B
    steps = ppb // npg
    ns = (past_len + T + SEL_BLOCK - 1) // SEL_BLOCK
    ns_pad = -(-ns // LANES) * LANES
    o_cmp, sel = pl.pallas_call(
        functools.partial(_nsa_sample_select_kernel, past_len=past_len, ns_pad=ns_pad),
        grid=(B,),
        in_specs=[pl.BlockSpec((1, T, q.shape[2]), lambda b: (b, 0, 0)),
                  pl.BlockSpec((1,) + cmp.shape[1:], lambda b: (b, 0, 0))],
        out_specs=[pl.BlockSpec((1, KV_HEADS, HPG * T, HEAD_DIM), lambda b: (b, 0, 0, 0)),
                   pl.BlockSpec((1, KV_HEADS * T, ns_pad), lambda b: (b, 0, 0))],
        out_shape=[jax.ShapeDtypeStruct((B, KV_HEADS, HPG * T, HEAD_DIM), F32),
                   jax.ShapeDtypeStruct((B, KV_HEADS * T, ns_pad), F32)],
        compiler_params=_cparams(("parallel",)),
        name="nsa_sample_select",
    )(q, cmp)

    def page_spec(p):
        return pl.BlockSpec((1, PAGE_SIZE, 2 * KV_WIDTH),
                            lambda b, c, pt: (pt[b * ppb + c * npg + p], 0, 1))

    per_b = lambda a: pl.BlockSpec((1,) + a.shape[1:], lambda b, c, pt: (b,) + (0,) * (a.ndim - 1))
    R = HPG * T
    return pl.pallas_call(
        functools.partial(_nsa_sample_attend_kernel, past_len=past_len, ns_pad=ns_pad),
        grid_spec=pltpu.PrefetchScalarGridSpec(
            num_scalar_prefetch=1,
            grid=(B, steps),
            in_specs=[page_spec(p) for p in range(npg)]
            + [per_b(q), per_b(sel), per_b(new_rows), per_b(band), per_b(o_cmp), per_b(gates)],
            out_specs=pl.BlockSpec((1, T, q.shape[2]), lambda b, c, pt: (b, 0, 0)),
            scratch_shapes=[pltpu.VMEM((KV_HEADS, R, LANES), F32),
                            pltpu.VMEM((KV_HEADS, R, LANES), F32),
                            pltpu.VMEM((KV_HEADS, R, HEAD_DIM), F32)]),
        out_shape=jax.ShapeDtypeStruct(q.shape, F32),
        compiler_params=_cparams(("parallel", "arbitrary")),
        name="nsa_sample_attend",
    )(page_ids, *([pages] * npg), q, sel, new_rows, band, o_cmp, gates)


def _conv_gate_kernel(b_ref, c_ref, h_ref, prev_ref, w_ref, o_ref, st_ref, carry_ref):
    i = pl.program_id(1)
    tt = b_ref.shape[0]

    @pl.when(i == 0)
    def _():
        carry_ref[...] = prev_ref[0]

    u = c_ref[...] * h_ref[...]
    row = lax.broadcasted_iota(jnp.int32, u.shape, 0)
    prev = carry_ref[...]
    u1 = jnp.where(row == 0, prev[7:8, :], pltpu.roll(u, 1, axis=0))
    u2 = jnp.where(row == 0, prev[6:7, :], jnp.where(row == 1, prev[7:8, :], pltpu.roll(u, 2, axis=0)))
    w = w_ref[...]
    conv = w[0:1, :] * u2 + w[1:2, :] * u1 + w[2:3, :] * u
    o_ref[...] = (b_ref[...] * conv).astype(o_ref.dtype)
    carry_ref[...] = u[tt - 8:, :]
    st_ref[0] = u[tt - 8:, :]


def conv_gate(proj, prev8, w_conv, B, T, *, tt, out_dtype):
    D = proj.shape[1] // 3
    nt = T // tt
    col = lambda k: pl.BlockSpec((tt, D), lambda b, i: (b * nt + i, k))
    return pl.pallas_call(
        _conv_gate_kernel,
        grid=(B, nt),
        in_specs=[col(0), col(1), col(2),
                  pl.BlockSpec((1, 8, D), lambda b, i: (b, 0, 0)),
                  pl.BlockSpec((8, D), lambda b, i: (0, 0))],
        out_specs=[pl.BlockSpec((tt, D), lambda b, i: (b * nt + i, 0)),
                   pl.BlockSpec((1, 8, D), lambda b, i: (b, 0, 0))],
        out_shape=[jax.ShapeDtypeStruct((B * T, D), out_dtype), jax.ShapeDtypeStruct((B, 8, D), F32)],
        scratch_shapes=[pltpu.VMEM((8, D), F32)],
        compiler_params=_cparams(("parallel", "arbitrary")),
        name="conv_gate",
    )(proj, proj, proj, prev8, w_conv)


def _cmlp_mix_kernel(u_ref, v_ref, g_ref, b_ref, ws_ref, bs_ref, o_ref, vo_ref):
    c = u_ref.shape[0]
    v = _ln(v_ref[...], g_ref[...], b_ref[...])
    vo_ref[...] = v
    vb = v.astype(BF16)
    dg = v.shape[1] // SG_GROUPS
    tril = lax.broadcasted_iota(jnp.int32, (c, c), 0) >= lax.broadcasted_iota(jnp.int32, (c, c), 1)
    for g in range(SG_GROUPS):
        ws = jnp.where(tril, ws_ref[g], 0.0).astype(BF16)
        mixed = _dot(ws, vb[:, g * dg:(g + 1) * dg]) + bs_ref[g]
        o_ref[:, g * dg:(g + 1) * dg] = (u_ref[:, g * dg:(g + 1) * dg] * mixed).astype(o_ref.dtype)


def cmlp_mix(proj, ln_g, ln_b, w_s, b_s, *, c, out_dtype):
    M = proj.shape[0]
    D = proj.shape[1] // 2
    const = lambda a: pl.BlockSpec(a.shape, lambda i: (0,) * a.ndim)
    ln_g = ln_g.reshape(1, D)
    ln_b = ln_b.reshape(1, D)
    return pl.pallas_call(
        _cmlp_mix_kernel,
        grid=(M // c,),
        in_specs=[pl.BlockSpec((c, D), lambda i: (i, 0)), pl.BlockSpec((c, D), lambda i: (i, 1)),
                  const(ln_g), const(ln_b), const(w_s), const(b_s)],
        out_specs=[pl.BlockSpec((c, D), lambda i: (i, 0)), pl.BlockSpec((c, D), lambda i: (i, 0))],
        out_shape=[jax.ShapeDtypeStruct((M, D), out_dtype), jax.ShapeDtypeStruct((M, D), F32)],
        compiler_params=_cparams(("parallel",)),
        name="cmlp_mix",
    )(proj, proj, ln_g, ln_b, w_s, b_s)


def _router_kernel(x_ref, w_ref, o_ref):
    logits = jnp.dot(x_ref[...], w_ref[...], preferred_element_type=F32, precision=lax.Precision.HIGHEST)
    lane = lax.broadcasted_iota(jnp.int32, logits.shape, 1)
    lane_f = lane.astype(F32)
    lg = jnp.where(lane < N_EXPERTS, logits, -jnp.inf)
    m1 = jnp.max(lg, axis=-1, keepdims=True)
    i1 = jnp.min(jnp.where(lg == m1, lane_f, float(LANES)), axis=-1, keepdims=True)
    lg2 = jnp.where(lane_f == i1, -jnp.inf, lg)
    m2 = jnp.max(lg2, axis=-1, keepdims=True)
    i2 = jnp.min(jnp.where(lg2 == m2, lane_f, float(LANES)), axis=-1, keepdims=True)
    e = jnp.exp(m2 - m1)
    den = 1.0 + e
    g1 = 1.0 / den
    g2 = e / den
    o_ref[...] = jnp.where(lane == 0, i1, jnp.where(lane == 1, i2,
                                                    jnp.where(lane == 2, g1, jnp.where(lane == 3, g2, 0.0))))


def router_top2(x, w_router, *, tm):
    M, K = x.shape
    w = jnp.pad(w_router, ((0, 0), (0, LANES - N_EXPERTS)))
    return pl.pallas_call(
        _router_kernel,
        grid=(M // tm,),
        in_specs=[pl.BlockSpec((tm, K), lambda i: (i, 0)), pl.BlockSpec((K, LANES), lambda i: (0, 0))],
        out_specs=pl.BlockSpec((tm, LANES), lambda i: (i, 0)),
        out_shape=jax.ShapeDtypeStruct((M, LANES), F32),
        compiler_params=_cparams(("parallel",)),
        name="moe_router",
    )(x, w)


def _gmm_swiglu_kernel(be_ref, nv_ref, x_ref, wg_ref, wu_ref, o_ref):
    @pl.when(pl.program_id(1) < nv_ref[0])
    def _():
        x = x_ref[...]
        g = _dot(x, wg_ref[0].astype(BF16))
        u = _dot(x, wu_ref[0].astype(BF16))
        o_ref[...] = (jax.nn.silu(g) * u).astype(o_ref.dtype)


def gmm_swiglu(xs, w_gu, block_e, n_valid, *, rb, tn):
    P, K = xs.shape
    F = w_gu.shape[2] // 2
    nf = F // tn
    row = lambda n, r, be, nv: jnp.minimum(r, nv[0] - 1)
    return pl.pallas_call(
        _gmm_swiglu_kernel,
        grid_spec=pltpu.PrefetchScalarGridSpec(
            num_scalar_prefetch=2,
            grid=(nf, P // rb),
            in_specs=[pl.BlockSpec((rb, K), lambda n, r, be, nv: (row(n, r, be, nv), 0)),
                      pl.BlockSpec((1, K, tn), lambda n, r, be, nv: (be[row(n, r, be, nv)], 0, n)),
                      pl.BlockSpec((1, K, tn), lambda n, r, be, nv: (be[row(n, r, be, nv)], 0, n + nf))],
            out_specs=pl.BlockSpec((rb, tn), lambda n, r, be, nv: (row(n, r, be, nv), n))),
        out_shape=jax.ShapeDtypeStruct((P, F), BF16),
        compiler_params=_cparams(("arbitrary", "arbitrary")),
        name="moe_gmm_swiglu",
    )(block_e, n_valid, xs, w_gu, w_gu)


def _gmm_down_kernel(be_ref, nv_ref, h_ref, w_ref, o_ref, acc_ref):
    k = pl.program_id(1)

    @pl.when(pl.program_id(0) < nv_ref[0])
    def _():
        @pl.when(k == 0)
        def _():
            acc_ref[...] = jnp.zeros_like(acc_ref)

        acc_ref[...] += _dot(h_ref[...], w_ref[0].astype(BF16))

        @pl.when(k == pl.num_programs(1) - 1)
        def _():
            o_ref[...] = acc_ref[...]


def gmm_down(h, w_down, block_e, n_valid, *, rb, tk):
    P, F = h.shape
    D = w_down.shape[2]
    row = lambda r, nv: jnp.minimum(r, nv[0] - 1)
    return pl.pallas_call(
        _gmm_down_kernel,
        grid_spec=pltpu.PrefetchScalarGridSpec(
            num_scalar_prefetch=2,
            grid=(P // rb, F // tk),
            in_specs=[pl.BlockSpec((rb, tk), lambda r, k, be, nv: (row(r, nv), k)),
                      pl.BlockSpec((1, tk, D), lambda r, k, be, nv: (be[row(r, nv)], k, 0))],
            out_specs=pl.BlockSpec((rb, D), lambda r, k, be, nv: (row(r, nv), 0)),
            scratch_shapes=[pltpu.VMEM((rb, D), F32)]),
        out_shape=jax.ShapeDtypeStruct((P, D), F32),
        compiler_params=_cparams(("arbitrary", "arbitrary")),
        name="moe_gmm_down",
    )(block_e, n_valid, h, w_down)


def moe_ffn(x, xb, w_router, w_gu, w_down, *, tm, rb):
    M = x.shape[0]
    r = router_top2(x, w_router, tm=tm)
    top_e = r[:, 0:2].astype(jnp.int32)
    gate = r[:, 2:4]
    A = M * TOP_K
    flat_e = top_e.reshape(A)
    onehot = (flat_e[:, None] == jnp.arange(N_EXPERTS, dtype=jnp.int32)[None, :]).astype(jnp.int32)
    pos_in_e = jnp.sum((jnp.cumsum(onehot, axis=0) - onehot) * onehot, axis=1)
    counts = jnp.sum(onehot, axis=0)
    padded = (counts + rb - 1) // rb * rb
    pend = jnp.cumsum(padded)
    pstart = pend - padded
    dest = (pstart[flat_e] + pos_in_e).astype(jnp.int32)
    n_blocks = -(-(A + N_EXPERTS * (rb - 1)) // rb)
    P = n_blocks * rb
    row_tok = jnp.zeros((P,), jnp.int32).at[dest].set(jnp.arange(A, dtype=jnp.int32) // TOP_K)
    block_e = jnp.minimum(jnp.searchsorted(pend, jnp.arange(n_blocks, dtype=jnp.int32) * rb, side="right"),
                          N_EXPERTS - 1).astype(jnp.int32)
    n_valid = (pend[-1:] // rb).astype(jnp.int32)
    xs = xb[row_tok]
    h = gmm_swiglu(xs, w_gu, block_e, n_valid, rb=rb, tn=512)
    y = gmm_down(h, w_down, block_e, n_valid, rb=rb, tk=512)
    dest2 = dest.reshape(M, TOP_K)
    return y[dest2[:, 0]] * gate[:, 0:1] + y[dest2[:, 1]] * gate[:, 1:2]


def _compress_weights(cmp_w1, cmp_pe, cmp_w2):
    s = CMP_STRIDE
    return jnp.concatenate([cmp_w1[0, :s], cmp_w1[0, s:], cmp_w1[1, :s], cmp_w1[1, s:]], axis=-1), cmp_pe, cmp_w2


def _gates_by_group(gates, M):
    g = gates[:, :3 * N_HEADS].reshape(M, 3, KV_HEADS, HPG).transpose(2, 0, 1, 3).reshape(KV_HEADS, M, 3 * HPG)
    return jnp.pad(g, ((0, 0), (0, 0), (0, LANES - 3 * HPG)))


def _gate_weights(w_in):
    c0 = N_HEADS * HEAD_DIM + 6 * KV_WIDTH
    return jnp.pad(w_in[:, c0:], ((0, 0), (0, LANES - 3 * N_HEADS)))


def _sub_block_view(rows):
    return rows.reshape(-1, PAGE_SIZE // CMP_STRIDE, CMP_STRIDE, 4 * KV_HEADS, HEAD_DIM)


def nsa_prompt_mixer(xb, w_in, cmp_w1, cmp_pe, cmp_w2, cos, sin, B, T, *, tm):
    M = B * T
    wcat, pe, w2 = _compress_weights(cmp_w1, cmp_pe, cmp_w2)
    q, rows, win = nsa_proj(xb, w_in, cos, sin, tm=tm, q_dtype=BF16)
    gates = linear(xb, _gate_weights(w_in), tm=tm, tn=LANES, act="sigmoid")
    page_ids = jnp.arange(M // PAGE_SIZE, dtype=jnp.int32)
    cmp = compress(_sub_block_view(rows), page_ids, B, wcat, pe, w2)
    o = nsa_prompt_attend(q, cmp, rows.reshape(B, T, 4 * KV_WIDTH), win.reshape(B, T, 2 * KV_WIDTH),
                          _gates_by_group(gates, M), B, T)
    wn = min(WINDOW, T)
    return (o, rows.reshape(B, T, 4, KV_HEADS, HEAD_DIM),
            win.reshape(B, T, 2, KV_HEADS, HEAD_DIM)[:, T - wn:])


def nsa_sample_mixer(xb, w_in, cmp_w1, cmp_pe, cmp_w2, cos, sin, pages, page_ids, win_buf, B, T):
    M = B * T
    past_len = page_ids.shape[0] // B * PAGE_SIZE
    wcat, pe, w2 = _compress_weights(cmp_w1, cmp_pe, cmp_w2)
    q, rows, win = nsa_proj(xb, w_in, cos, sin, tm=M, q_dtype=F32)
    gates = linear(xb, _gate_weights(w_in), tm=M, tn=LANES, act="sigmoid")
    cmp = compress(_sub_block_view(pages), page_ids, B, wcat, pe, w2)
    rows3 = rows.reshape(B, T, 4 * KV_WIDTH)
    new_rows = jnp.pad(rows3[:, :, 2 * KV_WIDTH:], ((0, 0), (0, LANES - T), (0, 0)))
    nbuf = win_buf.shape[1]
    band = jnp.concatenate([win_buf.reshape(B, nbuf, 2 * KV_WIDTH), win.reshape(B, T, 2 * KV_WIDTH)], axis=1)
    band_pad = jnp.pad(band, ((0, 0), (0, WINDOW + LANES - band.shape[1]), (0, 0)))
    o = nsa_sample_attend(q.reshape(B, T, -1), cmp, pages, page_ids, new_rows, band_pad,
                          gates.reshape(B, T, LANES), past_len)
    return (o.reshape(M, -1), rows.reshape(B, T, 4, KV_HEADS, HEAD_DIM),
            band[:, band.shape[1] - nbuf:].reshape(B, nbuf, 2, KV_HEADS, HEAD_DIM))


def conv_mixer(xb, prev, w_in, w_conv, B, T, *, tm, tt):
    keep = CONV_WIDTH - 1
    proj = linear(xb, w_in, tm=tm, tn=512)
    prev8 = jnp.pad(prev, ((0, 0), (8 - keep, 0), (0, 0)))
    w8 = jnp.pad(w_conv, ((0, 8 - CONV_WIDTH), (0, 0)))
    gated, st = conv_gate(proj, prev8, w8, B, T, tt=tt, out_dtype=BF16 if tt % 16 == 0 else F32)
    return gated, st[:, 8 - keep:]


def chunk_mlp_mixer(xb, w_in, ln_g, ln_b, w_s, b_s, B, T, *, tm):
    D = w_in.shape[1] // 2
    c = min(T, CHUNK)
    proj = linear(xb, w_in, tm=tm, tn=512, act="gelu")
    cp = max(c, 16)
    if cp != c:
        proj = jnp.pad(proj.reshape(B * T // c, c, 2 * D), ((0, 0), (0, cp - c), (0, 0))).reshape(-1, 2 * D)
    ws = jnp.pad(w_s[:, :c, :c], ((0, 0), (0, cp - c), (0, cp - c)))
    bs = jnp.pad(b_s[:, :c, None], ((0, 0), (0, cp - c), (0, 0)))
    mixed, v = cmlp_mix(proj, ln_g, ln_b, ws, bs, c=cp, out_dtype=BF16 if c == cp else F32)
    if cp != c:
        mixed = mixed.reshape(-1, cp, D)[:, :c].reshape(B * T, D)
        v = v.reshape(-1, cp, D)[:, :c].reshape(B * T, D)
    return mixed, v.reshape(B, T, D)[:, T - c:]


def kernel(x_prompt, x_sample, cache_nsa_kv, state_nsa_win, state_conv, page_table, ln_gain, ln_bias, nsa_w_in, nsa_cmp_w1, nsa_cmp_pe, nsa_cmp_w2, nsa_w_out, conv_w_in, conv_w, conv_w_out, cmlp_w_in, cmlp_ln_gain, cmlp_ln_bias, cmlp_w_s, cmlp_b_s, cmlp_w_out, ffn_w_gu, ffn_w_down, moe_router, moe_w_gu, moe_w_down):
    B, T, D = x_prompt.shape
    SB, ST, _ = x_sample.shape
    Mp, Ms = B * T, SB * ST
    n_pool = cache_nsa_kv.shape[1]
    n_pages = page_table.shape[1]
    past_len = n_pages * PAGE_SIZE
    tm_p, tm_s = 1024, Ms

    xp = x_prompt.reshape(Mp, D)
    xs = x_sample.reshape(Ms, D)
    xpb, xsb = xp, xs

    pos_p = jnp.tile(jnp.arange(T, dtype=jnp.int32), B)
    pos_s = jnp.tile(past_len + jnp.arange(ST, dtype=jnp.int32), SB)
    cos_p, sin_p = rope_tables(pos_p)
    cos_s, sin_s = rope_tables(pos_s)

    rows_p, rows_s, win_p, win_s = [], [], [], []
    conv_p, conv_s, v_p, v_s = [], [], [], []
    for i in range(DEPTH):
        m = i // 3
        kind = i % 3
        g0, b0 = ln_gain[i, 0], ln_bias[i, 0]
        if kind == 0:
            w = (nsa_w_in[m], nsa_cmp_w1[m], nsa_cmp_pe[m], nsa_cmp_w2[m])
            o, rows, win = nsa_prompt_mixer(xpb, *w, cos_p, sin_p, B, T, tm=tm_p)
            xp, xpb = linear_res_ln(o, nsa_w_out[m], xp, g0, b0, tm=512, tk=512)
            rows_p.append(rows)
            win_p.append(win)
            pages = cache_nsa_kv.reshape(-1, PAGE_SIZE, 4 * KV_WIDTH)
            page_ids = (m * n_pool + page_table).reshape(-1).astype(jnp.int32)
            o, rows, win = nsa_sample_mixer(xsb, *w, cos_s, sin_s, pages, page_ids, state_nsa_win[m], SB, ST)
            xs, xsb = linear_res_ln(o, nsa_w_out[m], xs, g0, b0, tm=tm_s, tk=512)
            rows_s.append(rows)
            win_s.append(win)
        elif kind == 1:
            w = (conv_w_in[m], conv_w[m])
            gated, st = conv_mixer(xpb, jnp.zeros((B, CONV_WIDTH - 1, D), F32), *w, B, T, tm=tm_p, tt=256)
            xp, xpb = linear_res_ln(gated, conv_w_out[m], xp, g0, b0, tm=512, tk=512)
            conv_p.append(st)
            gated, st = conv_mixer(xsb, state_conv[m], *w, SB, ST, tm=tm_s, tt=ST)
            xs, xsb = linear_res_ln(gated, conv_w_out[m], xs, g0, b0, tm=tm_s, tk=512)
            conv_s.append(st)
        else:
            w = (cmlp_w_in[m], cmlp_ln_gain[m], cmlp_ln_bias[m], cmlp_w_s[m], cmlp_b_s[m])
            mixed, v = chunk_mlp_mixer(xpb, *w, B, T, tm=tm_p)
            xp, xpb = linear_res_ln(mixed, cmlp_w_out[m], xp, g0, b0, tm=512, tk=512)
            v_p.append(v)
            mixed, v = chunk_mlp_mixer(xsb, *w, SB, ST, tm=tm_s)
            xs, xsb = linear_res_ln(mixed, cmlp_w_out[m], xs, g0, b0, tm=tm_s, tk=512)
            v_s.append(v)
        g1, b1 = ln_gain[i, 1], ln_bias[i, 1]
        f = i // 2
        if i % 2 == 0:
            h = linear_swiglu(xpb, ffn_w_gu[f], tm=tm_p, tn=512)
            xp, xpb = linear_res_ln(h, ffn_w_down[f], xp, g1, b1, tm=512, tk=512)
            h = linear_swiglu(xsb, ffn_w_gu[f], tm=tm_s, tn=512)
            xs, xsb = linear_res_ln(h, ffn_w_down[f], xs, g1, b1, tm=tm_s, tk=512)
        else:
            fp = moe_ffn(xp, xpb, moe_router[f], moe_w_gu[f], moe_w_down[f], tm=tm_p, rb=512)
            xp, xpb = res_ln(xp, fp, g1, b1, tm=512)
            fs = moe_ffn(xs, xsb, moe_router[f], moe_w_gu[f], moe_w_down[f], tm=tm_s, rb=128)
            xs, xsb = res_ln(xs, fs, g1, b1, tm=tm_s)
    return (xp.reshape(B, T, D), xs.reshape(SB, ST, D),
            jnp.stack(rows_p), jnp.stack(rows_s), jnp.stack(win_p), jnp.stack(win_s),
            jnp.stack(conv_p), jnp.stack(conv_s), jnp.stack(v_p), jnp.stack(v_s))
```

```python
import functools

import jax
import jax.numpy as jnp
from jax import lax
from jax.experimental import pallas as pl
from jax.experimental.pallas import tpu as pltpu

D_MODEL = 2048
DEPTH = 4
PAGE_SIZE = 128
N_HEADS = 16
HEAD_DIM = 128
KV_HEADS = 4
HPG = N_HEADS // KV_HEADS
KV_WIDTH = KV_HEADS * HEAD_DIM
CMP_BLOCK = 32
CMP_STRIDE = 16
SEL_BLOCK = 64
SEL_TOPK = 16
WINDOW = 512
ROPE_THETA = 10000.0
FORCE_BONUS = 1e4
CONV_WIDTH = 3
CHUNK = 128
SG_GROUPS = 8
N_EXPERTS = 8
TOP_K = 2
LN_EPS = 1e-5
DEEPNORM_ALPHA = (2 * DEPTH) ** 0.25

LANES = 128
PAGES_PER_STEP = 8
NEG = -1e30
Q_SCALE = HEAD_DIM ** -0.5 * 1.4426950408889634
VMEM_LIMIT = 56 * 1024 * 1024

F32 = jnp.float32
BF16 = jnp.bfloat16


def _cparams(sem):
    return pltpu.CompilerParams(dimension_semantics=sem, vmem_limit_bytes=VMEM_LIMIT)


def _ln(y, g, b):
    mu = jnp.mean(y, axis=-1, keepdims=True)
    d = y - mu
    var = jnp.mean(d * d, axis=-1, keepdims=True)
    return d * lax.rsqrt(var + LN_EPS) * g + b


def _dot(a, b):
    return jnp.dot(a, b, preferred_element_type=F32)


def _dot_nt(a, b, precision=None):
    return lax.dot_general(a, b, (((1,), (1,)), ((), ())), preferred_element_type=F32,
                           precision=precision)


def _linear_kernel(x_ref, w_ref, o_ref, *, act):
    y = _dot(x_ref[...].astype(BF16), w_ref[...].astype(BF16))
    if act == "gelu":
        y = jax.nn.gelu(y)
    elif act == "sigmoid":
        y = jax.nn.sigmoid(y)
    o_ref[...] = y.astype(o_ref.dtype)


def linear(x, w, *, tm, tn, act=None, out_dtype=F32):
    M, K = x.shape
    N = w.shape[1]
    return pl.pallas_call(
        functools.partial(_linear_kernel, act=act),
        grid=(M // tm, N // tn),
        in_specs=[pl.BlockSpec((tm, K), lambda i, j: (i, 0)),
                  pl.BlockSpec((K, tn), lambda i, j: (0, j))],
        out_specs=pl.BlockSpec((tm, tn), lambda i, j: (i, j)),
        out_shape=jax.ShapeDtypeStruct((M, N), out_dtype),
        compiler_params=_cparams(("parallel", "arbitrary")),
        name="linear",
    )(x, w)


def _swiglu_kernel(x_ref, wg_ref, wu_ref, o_ref):
    x = x_ref[...].astype(BF16)
    g = _dot(x, wg_ref[...].astype(BF16))
    u = _dot(x, wu_ref[...].astype(BF16))
    o_ref[...] = (jax.nn.silu(g) * u).astype(o_ref.dtype)


def linear_swiglu(x, w_gu, *, tm, tn):
    M, K = x.shape
    F = w_gu.shape[1] // 2
    nf = F // tn
    return pl.pallas_call(
        _swiglu_kernel,
        grid=(M // tm, nf),
        in_specs=[pl.BlockSpec((tm, K), lambda i, j: (i, 0)),
                  pl.BlockSpec((K, tn), lambda i, j: (0, j)),
                  pl.BlockSpec((K, tn), lambda i, j: (0, j + nf))],
        out_specs=pl.BlockSpec((tm, tn), lambda i, j: (i, j)),
        out_shape=jax.ShapeDtypeStruct((M, F), BF16),
        compiler_params=_cparams(("parallel", "arbitrary")),
        name="linear_swiglu",
    )(x, w_gu, w_gu)


def _linear_res_ln_kernel(h_ref, w_ref, res_ref, g_ref, b_ref, o_ref, ob_ref, acc_ref):
    k = pl.program_id(1)

    @pl.when(k == 0)
    def _():
        acc_ref[...] = jnp.zeros_like(acc_ref)

    acc_ref[...] += _dot(h_ref[...].astype(BF16), w_ref[...].astype(BF16))

    @pl.when(k == pl.num_programs(1) - 1)
    def _():
        y = _ln(DEEPNORM_ALPHA * res_ref[...] + acc_ref[...], g_ref[...], b_ref[...])
        o_ref[...] = y
        ob_ref[...] = y.astype(BF16)


def linear_res_ln(h, w, res, gain, bias, *, tm, tk):
    M, K = h.shape
    N = w.shape[1]
    return pl.pallas_call(
        _linear_res_ln_kernel,
        grid=(M // tm, K // tk),
        in_specs=[pl.BlockSpec((tm, tk), lambda i, k: (i, k)),
                  pl.BlockSpec((tk, N), lambda i, k: (k, 0)),
                  pl.BlockSpec((tm, N), lambda i, k: (i, 0)),
                  pl.BlockSpec((1, N), lambda i, k: (0, 0)),
                  pl.BlockSpec((1, N), lambda i, k: (0, 0))],
        out_specs=[pl.BlockSpec((tm, N), lambda i, k: (i, 0)),
                   pl.BlockSpec((tm, N), lambda i, k: (i, 0))],
        out_shape=[jax.ShapeDtypeStruct((M, N), F32), jax.ShapeDtypeStruct((M, N), BF16)],
        scratch_shapes=[pltpu.VMEM((tm, N), F32)],
        compiler_params=_cparams(("parallel", "arbitrary")),
        name="linear_res_ln",
    )(h, w, res, gain.reshape(1, N), bias.reshape(1, N))


def _rope_tile(y, cos, sin):
    outs = []
    for c in range(y.shape[1] // HEAD_DIM):
        yc = y[:, c * HEAD_DIM:(c + 1) * HEAD_DIM]
        outs.append(yc * cos + pltpu.roll(yc, HEAD_DIM // 2, axis=1) * sin)
    return jnp.concatenate(outs, axis=1)


def _nsa_proj_kernel(x_ref, w_ref, cos_ref, sin_ref, q_ref, rows_ref, win_ref):
    j = pl.program_id(1)
    y = _dot(x_ref[...].astype(BF16), w_ref[...].astype(BF16))
    cos = cos_ref[...]
    sin = sin_ref[...]

    @pl.when(j < 4)
    def _():
        q_ref[...] = (_rope_tile(y, cos, sin) * Q_SCALE).astype(q_ref.dtype)

    @pl.when((j == 4) | (j == 6))
    def _():
        rows_ref[...] = _rope_tile(y, cos, sin)

    @pl.when((j == 5) | (j == 7))
    def _():
        rows_ref[...] = y

    @pl.when(j == 8)
    def _():
        win_ref[...] = _rope_tile(y, cos, sin)

    @pl.when(j == 9)
    def _():
        win_ref[...] = y


def nsa_proj(x, w_in, cos, sin, *, tm, q_dtype):
    M, K = x.shape
    tn = KV_WIDTH
    qd = N_HEADS * HEAD_DIM
    return pl.pallas_call(
        _nsa_proj_kernel,
        grid=(M // tm, 10),
        in_specs=[pl.BlockSpec((tm, K), lambda i, j: (i, 0)),
                  pl.BlockSpec((K, tn), lambda i, j: (0, j)),
                  pl.BlockSpec((tm, HEAD_DIM), lambda i, j: (i, 0)),
                  pl.BlockSpec((tm, HEAD_DIM), lambda i, j: (i, 0))],
        out_specs=[pl.BlockSpec((tm, tn), lambda i, j: (i, jnp.minimum(j, 3))),
                   pl.BlockSpec((tm, tn), lambda i, j: (i, jnp.clip(j - 4, 0, 3))),
                   pl.BlockSpec((tm, tn), lambda i, j: (i, jnp.clip(j - 8, 0, 1)))],
        out_shape=[jax.ShapeDtypeStruct((M, qd), q_dtype),
                   jax.ShapeDtypeStruct((M, 4 * KV_WIDTH), F32),
                   jax.ShapeDtypeStruct((M, 2 * KV_WIDTH), F32)],
        compiler_params=_cparams(("parallel", "arbitrary")),
        name="nsa_proj",
    )(x, w_in, cos, sin)


def rope_tables(pos):
    half = HEAD_DIM // 2
    inv = ROPE_THETA ** (-jnp.arange(half, dtype=F32) / half)
    ang = pos.astype(F32)[:, None] * inv[None, :]
    cos = jnp.cos(ang)
    sin = jnp.sin(ang)
    return jnp.concatenate([cos, cos], -1), jnp.concatenate([-sin, sin], -1)


def _compress_kernel(pt_ref, *refs):
    npg = PAGES_PER_STEP
    page_refs = refs[:npg]
    wcat_ref, pe_ref, w2_ref, o_ref, carry_ref = refs[npg:]
    j = pl.program_id(1)
    nch = 2 * KV_HEADS
    nsub = PAGE_SIZE // CMP_STRIDE
    R = npg * nsub * nch

    @pl.when(j == 0)
    def _():
        carry_ref[...] = jnp.zeros_like(carry_ref)

    def rows_at(l):
        return jnp.concatenate([page_refs[p][0, :, l].reshape(nsub * nch, HEAD_DIM) for p in range(npg)],
                               axis=0).astype(BF16)

    acc = jnp.zeros((R, 4 * HEAD_DIM), F32)
    for l2 in range(CMP_STRIDE // 2):
        x = jnp.concatenate([rows_at(2 * l2), rows_at(2 * l2 + 1)], axis=1)
        acc = acc + _dot(x, wcat_ref[l2])
    outs = []
    for t in range(2):
        a = acc[:, 2 * t * HEAD_DIM:(2 * t + 1) * HEAD_DIM]
        b = acc[:, (2 * t + 1) * HEAD_DIM:2 * (t + 1) * HEAD_DIM]
        a_shift = jnp.concatenate([carry_ref[t], a[:R - nch]], axis=0)
        carry_ref[t] = a[R - nch:]
        h = a_shift + b + pe_ref[t:t + 1, :]
        outs.append(_dot(jax.nn.gelu(h).astype(BF16), w2_ref[t]))
    is_k = lax.broadcasted_iota(jnp.int32, (R, HEAD_DIM), 0) % nch < KV_HEADS
    o_ref[0] = jnp.where(is_k, outs[0], outs[1])


def _compress_pe_kernel(w1_ref, pe_ref, o_ref):
    for t in range(2):
        acc = jnp.zeros((HEAD_DIM, HEAD_DIM), F32)
        for l in range(CMP_BLOCK):
            acc = acc + w1_ref[t, l] * pe_ref[t, l]
        o_ref[t:t + 1, :] = jnp.sum(acc, axis=0, keepdims=True)


def compress_pe_term(cmp_w1, cmp_pe):
    return pl.pallas_call(
        _compress_pe_kernel,
        out_shape=jax.ShapeDtypeStruct((2, HEAD_DIM), F32),
        compiler_params=_cparams(()),
        name="nsa_compress_pe",
    )(cmp_w1, cmp_pe[..., None])


def compress(pages, page_ids, n_batch, wcat, pe, w2):
    npg = PAGES_PER_STEP
    nch = 2 * KV_HEADS
    nsub = PAGE_SIZE // CMP_STRIDE
    ppb = page_ids.shape[0] // n_batch
    steps = ppb // npg
    R = npg * nsub * nch

    def page_spec(p):
        return pl.BlockSpec((1, nsub, CMP_STRIDE, nch, HEAD_DIM),
                            lambda b, j, pt: (pt[b * ppb + j * npg + p], 0, 0, 0, 0))

    const = lambda shape: pl.BlockSpec(shape, lambda b, j, pt: (0,) * len(shape))
    out = pl.pallas_call(
        _compress_kernel,
        grid_spec=pltpu.PrefetchScalarGridSpec(
            num_scalar_prefetch=1,
            grid=(n_batch, steps),
            in_specs=[page_spec(p) for p in range(npg)]
            + [const(wcat.shape), const(pe.shape), const(w2.shape)],
            out_specs=pl.BlockSpec((1, R, HEAD_DIM), lambda b, j, pt: (b, j, 0)),
            scratch_shapes=[pltpu.VMEM((2, nch, HEAD_DIM), F32)]),
        out_shape=jax.ShapeDtypeStruct((n_batch, steps * R, HEAD_DIM), F32),
        compiler_params=_cparams(("parallel", "arbitrary")),
        name="nsa_compress",
    )(page_ids, *([pages] * npg), wcat, pe, w2)
    return out.reshape(n_batch, steps * npg * nsub, nch * HEAD_DIM)


def _softmax_rows(s, mask):
    sm = jnp.where(mask, s, NEG)
    m = jnp.max(sm, axis=-1, keepdims=True)
    p = jnp.where(mask, jnp.exp2(sm - m), 0.0)
    return p * (1.0 / jnp.maximum(jnp.sum(p, axis=-1, keepdims=True), 1e-30))


def _online_step(s, mask, v, m, l, acc):
    sm = jnp.where(mask, s, NEG)
    m_new = jnp.maximum(m, jnp.max(sm, axis=-1, keepdims=True))
    p = jnp.where(mask, jnp.exp2(sm - m_new), 0.0)
    alpha = jnp.exp2(m - m_new)
    l = alpha * l + jnp.sum(p, axis=-1, keepdims=True)
    acc = alpha * acc + _dot(p.astype(BF16), v)
    return m_new, l, acc


def _mask_bias(mask):
    return jnp.where(mask, 0.0, NEG).astype(F32)


def _online_step_biased(s, bias, v, m, l, acc):
    R, C = s.shape
    T = bias.shape[0]
    sm = (s.reshape(R // T, T, C) + bias[None]).reshape(R, C)
    m_new = jnp.maximum(m, jnp.max(sm, axis=-1, keepdims=True))
    p = jnp.exp2(sm - m_new)
    alpha = jnp.exp2(m - m_new)
    l = alpha * l + jnp.sum(p, axis=-1, keepdims=True)
    acc = alpha * acc + _dot(p.astype(BF16), v)
    return m_new, l, acc


def _pool_matrix(n_rows, n_cols, transposed):
    shape = (n_cols, n_rows) if transposed else (n_rows, n_cols)
    n_ax, j_ax = (1, 0) if transposed else (0, 1)
    d = lax.broadcasted_iota(jnp.int32, shape, n_ax) - 4 * lax.broadcasted_iota(jnp.int32, shape, j_ax)
    return jnp.where((d == 0) | (d == 4), 1.0, jnp.where((d > 0) & (d < 4), 2.0, 0.0)).astype(F32)


TQ = 128
SEL_CHUNK = 512


def _nsa_prompt_kernel(q_ref, kc_ref, vc_ref, ks_ref, vs_ref, kw_ref, vw_ref, gate_ref, o_ref):
    qi = pl.program_id(2)
    t0 = qi * TQ
    R = HPG * TQ
    q = q_ref[...]
    qs = jnp.concatenate([q[:, h * HEAD_DIM:(h + 1) * HEAD_DIM] for h in range(HPG)], axis=0)
    t_row = t0 + lax.broadcasted_iota(jnp.int32, (TQ, 1), 0)
    t_rows = jnp.concatenate([t_row] * HPG, axis=0)

    nc = kc_ref.shape[1]
    s = _dot_nt(qs, kc_ref[0].astype(BF16))
    n_idx = lax.broadcasted_iota(jnp.int32, (1, nc), 1)
    cmask = (n_idx >= 1) & (CMP_STRIDE * n_idx + (CMP_STRIDE - 1) <= t_rows)
    p_cmp = _softmax_rows(s, cmask)
    o_cmp = _dot(p_cmp.astype(BF16), vc_ref[0].astype(BF16))
    pg = p_cmp[0:TQ]
    for h in range(1, HPG):
        pg = pg + p_cmp[h * TQ:(h + 1) * TQ]

    ns = ks_ref.shape[1] // SEL_BLOCK
    p_slc = _dot_nt(_pool_matrix(nc, ns, True), pg, precision=lax.Precision.HIGHEST)
    blk = lax.broadcasted_iota(jnp.int32, (ns, TQ), 0)
    cur = (t0 + lax.broadcasted_iota(jnp.int32, (ns, TQ), 1)) // SEL_BLOCK
    avail = blk <= cur
    forced = (blk == 0) | (blk == cur) | (blk == cur - 1)
    score = jnp.where(avail, p_slc + jnp.where(forced, FORCE_BONUS, 0.0), -jnp.inf)
    rank = jnp.zeros((ns, TQ), jnp.int32)
    for jj in range(ns):
        row = score[jj:jj + 1, :]
        beats = (row > score) | ((row == score) & (blk > jj))
        rank = rank + beats.astype(jnp.int32)
    sel_t = jnp.where((rank < SEL_TOPK) & avail, 1.0, 0.0).astype(F32)
    sel_t = jnp.concatenate([sel_t, jnp.zeros((LANES - ns, TQ), F32)], axis=0)
    sel = sel_t.T.astype(BF16)

    def sel_body(c, carry):
        k0 = pl.multiple_of(c * SEL_CHUNK, SEL_CHUNK)
        key = k0 + lax.broadcasted_iota(jnp.int32, (LANES, SEL_CHUNK), 1)
        expand = (lax.broadcasted_iota(jnp.int32, (LANES, SEL_CHUNK), 0) == key // SEL_BLOCK)
        picked = _dot(sel, expand.astype(BF16)) > 0.5
        kpos = k0 + lax.broadcasted_iota(jnp.int32, (1, SEL_CHUNK), 1)
        bias = _mask_bias(picked & (kpos <= t_row))
        k = ks_ref[0, pl.ds(k0, SEL_CHUNK), :].astype(BF16)
        v = vs_ref[0, pl.ds(k0, SEL_CHUNK), :].astype(BF16)
        return _online_step_biased(_dot_nt(qs, k), bias, v, *carry)

    init = (jnp.full((R, 1), NEG, F32), jnp.zeros((R, 1), F32), jnp.zeros((R, HEAD_DIM), F32))
    n_chunks = (t0 + TQ + SEL_CHUNK - 1) // SEL_CHUNK
    _, l, acc = lax.fori_loop(0, n_chunks, sel_body, init)
    o_sel = acc * (1.0 / l)

    band = WINDOW + TQ
    start = pl.multiple_of(jnp.maximum(t0 - WINDOW, 0), TQ)
    kpos = start + lax.broadcasted_iota(jnp.int32, (1, band), 1)
    bias = _mask_bias((kpos <= t_row) & (kpos > t_row - WINDOW))
    s = _dot_nt(qs, kw_ref[0, pl.ds(start, band), :].astype(BF16))
    _, l, acc = _online_step_biased(s, bias, vw_ref[0, pl.ds(start, band), :].astype(BF16), *init)
    o_win = acc * (1.0 / l)

    gates = gate_ref[0]
    for h in range(HPG):
        rows = slice(h * TQ, (h + 1) * TQ)
        o = (gates[:, h:h + 1] * o_cmp[rows] + gates[:, HPG + h:HPG + h + 1] * o_sel[rows]
             + gates[:, 2 * HPG + h:2 * HPG + h + 1] * o_win[rows])
        o_ref[:, h * HEAD_DIM:(h + 1) * HEAD_DIM] = o.astype(o_ref.dtype)


def nsa_prompt_attend(q, cmp, rows, win, gates_g, B, T):
    nq = T // TQ
    kv = lambda col0: pl.BlockSpec((1, T, HEAD_DIM), lambda b, g, i: (b, 0, col0 + g))
    cmp_spec = lambda col0: pl.BlockSpec((1, cmp.shape[1], HEAD_DIM), lambda b, g, i: (b, 0, col0 + g))
    return pl.pallas_call(
        _nsa_prompt_kernel,
        grid=(B, KV_HEADS, nq),
        in_specs=[pl.BlockSpec((TQ, HPG * HEAD_DIM), lambda b, g, i: (b * nq + i, g)),
                  cmp_spec(0), cmp_spec(KV_HEADS),
                  kv(2 * KV_HEADS), kv(3 * KV_HEADS), kv(0), kv(KV_HEADS),
                  pl.BlockSpec((1, TQ, LANES), lambda b, g, i: (g, b * nq + i, 0))],
        out_specs=pl.BlockSpec((TQ, HPG * HEAD_DIM), lambda b, g, i: (b * nq + i, g)),
        out_shape=jax.ShapeDtypeStruct((B * T, N_HEADS * HEAD_DIM), BF16),
        compiler_params=_cparams(("parallel", "parallel", "arbitrary")),
        name="nsa_prompt_attend",
    )(q, cmp, cmp, rows, rows, win, win, gates_g)


def _nsa_sample_select_kernel(q_ref, cmp_ref, ocmp_ref, sel_ref, *, past_len, ns_pad):
    T = q_ref.shape[1]
    R = HPG * T
    nc = cmp_ref.shape[1]
    q = q_ref[0]
    t_pos = past_len + lax.broadcasted_iota(jnp.int32, (T, 1), 0)
    t_rows = jnp.concatenate([t_pos] * HPG, axis=0)
    n_idx = lax.broadcasted_iota(jnp.int32, (1, nc), 1)
    cmask = (n_idx >= 1) & (CMP_STRIDE * n_idx + (CMP_STRIDE - 1) <= t_rows)
    pool = _pool_matrix(nc, ns_pad, False)
    pgs = []
    for g in range(KV_HEADS):
        qs = jnp.concatenate(
            [q[:, (g * HPG + h) * HEAD_DIM:(g * HPG + h + 1) * HEAD_DIM] for h in range(HPG)],
            axis=0).astype(BF16)
        kc = cmp_ref[0, :, g * HEAD_DIM:(g + 1) * HEAD_DIM].astype(BF16)
        vc = cmp_ref[0, :, (KV_HEADS + g) * HEAD_DIM:(KV_HEADS + g + 1) * HEAD_DIM].astype(BF16)
        p = _softmax_rows(_dot_nt(qs, kc), cmask)
        ocmp_ref[0, g] = _dot(p.astype(BF16), vc)
        pg = p[0:T]
        for h in range(1, HPG):
            pg = pg + p[h * T:(h + 1) * T]
        pgs.append(pg)
    pg = jnp.concatenate(pgs, axis=0)
    p_slc = jnp.dot(pg, pool, preferred_element_type=F32, precision=lax.Precision.HIGHEST)
    blk = lax.broadcasted_iota(jnp.int32, (KV_HEADS * T, ns_pad), 1)
    cur = jnp.concatenate([t_pos] * KV_HEADS, axis=0) // SEL_BLOCK
    avail = blk <= cur
    forced = (blk == 0) | (blk == cur) | (blk == cur - 1)
    score = jnp.where(avail, p_slc + jnp.where(forced, FORCE_BONUS, 0.0), -jnp.inf)
    sel = jnp.zeros(score.shape, F32)
    blk_f = blk.astype(F32)
    for _ in range(SEL_TOPK):
        m = jnp.max(score, axis=-1, keepdims=True)
        first = jnp.min(jnp.where(score == m, blk_f, float(ns_pad)), axis=-1, keepdims=True)
        pick = blk_f == first
        sel = jnp.where(pick & (m > -jnp.inf), 1.0, sel)
        score = jnp.where(pick, -jnp.inf, score)
    sel_ref[0] = sel


def _nsa_sample_attend_kernel(pt_ref, *refs, past_len, ns_pad):
    npg = PAGES_PER_STEP
    page_refs = refs[:npg]
    (q_ref, sel_ref, new_ref, band_ref, ocmp_ref, gate_ref, o_ref, m_ref, l_ref, acc_ref) = refs[npg:]
    c = pl.program_id(1)
    T = q_ref.shape[1]
    R = HPG * T
    chunk = npg * PAGE_SIZE
    q = q_ref[0]
    t_pos = past_len + lax.broadcasted_iota(jnp.int32, (T, 1), 0)
    t_rows = jnp.concatenate([t_pos] * HPG, axis=0)

    @pl.when(c == 0)
    def _():
        m_ref[...] = jnp.full(m_ref.shape, NEG, F32)
        l_ref[...] = jnp.zeros_like(l_ref)
        acc_ref[...] = jnp.zeros_like(acc_ref)

    def qs_of(g):
        return jnp.concatenate(
            [q[:, (g * HPG + h) * HEAD_DIM:(g * HPG + h + 1) * HEAD_DIM] for h in range(HPG)],
            axis=0).astype(BF16)

    def picked_masks(first_key, n_keys):
        key = first_key + lax.broadcasted_iota(jnp.int32, (ns_pad, n_keys), 1)
        expand = lax.broadcasted_iota(jnp.int32, (ns_pad, n_keys), 0) == key // SEL_BLOCK
        hit = _dot(sel_ref[0].astype(BF16), expand.astype(BF16))
        kpos = first_key + lax.broadcasted_iota(jnp.int32, (1, n_keys), 1)
        out = []
        for g in range(KV_HEADS):
            m1 = (hit[g * T:(g + 1) * T] > 0.5) & (kpos <= t_pos)
            out.append(jnp.concatenate([m1] * HPG, axis=0))
        return out

    def update(g, s, mask, v):
        m, l, acc = _online_step(s, mask, v, m_ref[g][:, 0:1], l_ref[g][:, 0:1], acc_ref[g])
        m_ref[g] = jnp.broadcast_to(m, (R, LANES))
        l_ref[g] = jnp.broadcast_to(l, (R, LANES))
        acc_ref[g] = acc

    masks = picked_masks(c * chunk, chunk)
    for g in range(KV_HEADS):
        qs = qs_of(g)
        k = jnp.concatenate([page_refs[p][0, :, g, :] for p in range(npg)], axis=0).astype(BF16)
        v = jnp.concatenate([page_refs[p][0, :, KV_HEADS + g, :] for p in range(npg)], axis=0).astype(BF16)
        update(g, _dot_nt(qs, k), masks[g], v)

    @pl.when(c == pl.num_programs(1) - 1)
    def _():
        n_new = new_ref.shape[1]
        n_band = band_ref.shape[1]
        gates = gate_ref[0]
        new_masks = picked_masks(past_len, n_new)
        for g in range(KV_HEADS):
            qs = qs_of(g)
            k = new_ref[0, :, g * HEAD_DIM:(g + 1) * HEAD_DIM].astype(BF16)
            v = new_ref[0, :, KV_WIDTH + g * HEAD_DIM:KV_WIDTH + (g + 1) * HEAD_DIM].astype(BF16)
            update(g, _dot_nt(qs, k), new_masks[g], v)
            o_sel = acc_ref[g] * (1.0 / jnp.maximum(l_ref[g][:, 0:1], 1e-30))
            kpos = (past_len - WINDOW) + lax.broadcasted_iota(jnp.int32, (1, n_band), 1)
            wmask = (kpos <= t_rows) & (kpos > t_rows - WINDOW)
            kw = band_ref[0, :, g * HEAD_DIM:(g + 1) * HEAD_DIM].astype(BF16)
            vw = band_ref[0, :, KV_WIDTH + g * HEAD_DIM:KV_WIDTH + (g + 1) * HEAD_DIM].astype(BF16)
            p_win = _softmax_rows(_dot_nt(qs, kw), wmask)
            o_win = _dot(p_win.astype(BF16), vw)
            o_cmp = ocmp_ref[0, g]
            for h in range(HPG):
                rows = slice(h * T, (h + 1) * T)
                col = g * HPG + h
                o = (gates[:, col:col + 1] * o_cmp[rows]
                     + gates[:, N_HEADS + col:N_HEADS + col + 1] * o_sel[rows]
                     + gates[:, 2 * N_HEADS + col:2 * N_HEADS + col + 1] * o_win[rows])
                o_ref[0, :, col * HEAD_DIM:(col + 1) * HEAD_DIM] = o


def nsa_sample_attend(q, cmp, pages, page_ids, new_rows, band, gates, past_len):
    B, T, _ = q.shape
    npg = PAGES_PER_STEP
    ppb = page_ids.shape[0] // B
    steps = ppb // npg
    ns = (past_len + T + SEL_BLOCK - 1) // SEL_BLOCK
    ns_pad = -(-ns // LANES) * LANES
    o_cmp, sel = pl.pallas_call(
        functools.partial(_nsa_sample_select_kernel, past_len=past_len, ns_pad=ns_pad),
        grid=(B,),
        in_specs=[pl.BlockSpec((1, T, q.shape[2]), lambda b: (b, 0, 0)),
                  pl.BlockSpec((1,) + cmp.shape[1:], lambda b: (b, 0, 0))],
        out_specs=[pl.BlockSpec((1, KV_HEADS, HPG * T, HEAD_DIM), lambda b: (b, 0, 0, 0)),
                   pl.BlockSpec((1, KV_HEADS * T, ns_pad), lambda b: (b, 0, 0))],
        out_shape=[jax.ShapeDtypeStruct((B, KV_HEADS, HPG * T, HEAD_DIM), F32),
                   jax.ShapeDtypeStruct((B, KV_HEADS * T, ns_pad), F32)],
        compiler_params=_cparams(("parallel",)),
        name="nsa_sample_select",
    )(q, cmp)

    def page_spec(p):
        return pl.BlockSpec((1, PAGE_SIZE, 2 * KV_HEADS, HEAD_DIM),
                            lambda b, c, pt: (pt[b * ppb + c * npg + p], 0, 1, 0))

    per_b = lambda a: pl.BlockSpec((1,) + a.shape[1:], lambda b, c, pt: (b,) + (0,) * (a.ndim - 1))
    R = HPG * T
    return pl.pallas_call(
        functools.partial(_nsa_sample_attend_kernel, past_len=past_len, ns_pad=ns_pad),
        grid_spec=pltpu.PrefetchScalarGridSpec(
            num_scalar_prefetch=1,
            grid=(B, steps),
            in_specs=[page_spec(p) for p in range(npg)]
            + [per_b(q), per_b(sel), per_b(new_rows), per_b(band), per_b(o_cmp), per_b(gates)],
            out_specs=pl.BlockSpec((1, T, q.shape[2]), lambda b, c, pt: (b, 0, 0)),
            scratch_shapes=[pltpu.VMEM((KV_HEADS, R, LANES), F32),
                            pltpu.VMEM((KV_HEADS, R, LANES), F32),
                            pltpu.VMEM((KV_HEADS, R, HEAD_DIM), F32)]),
        out_shape=jax.ShapeDtypeStruct(q.shape, F32),
        compiler_params=_cparams(("parallel", "arbitrary")),
        name="nsa_sample_attend",
    )(page_ids, *([pages] * npg), q, sel, new_rows, band, o_cmp, gates)


def _conv_gate_kernel(b_ref, c_ref, h_ref, prev_ref, w_ref, o_ref, st_ref, carry_ref):
    i = pl.program_id(1)
    tt = b_ref.shape[0]

    @pl.when(i == 0)
    def _():
        carry_ref[...] = prev_ref[0]

    u = c_ref[...] * h_ref[...]
    row = lax.broadcasted_iota(jnp.int32, u.shape, 0)
    prev = carry_ref[...]
    u1 = jnp.where(row == 0, prev[7:8, :], pltpu.roll(u, 1, axis=0))
    u2 = jnp.where(row == 0, prev[6:7, :], jnp.where(row == 1, prev[7:8, :], pltpu.roll(u, 2, axis=0)))
    w = w_ref[...]
    conv = w[0:1, :] * u2 + w[1:2, :] * u1 + w[2:3, :] * u
    o_ref[...] = (b_ref[...] * conv).astype(o_ref.dtype)
    carry_ref[...] = u[tt - 8:, :]
    st_ref[0] = u[tt - 8:, :]


def conv_gate(proj, prev8, w_conv, B, T, *, tt, out_dtype):
    D = proj.shape[1] // 3
    nt = T // tt
    col = lambda k: pl.BlockSpec((tt, D), lambda b, i: (b * nt + i, k))
    return pl.pallas_call(
        _conv_gate_kernel,
        grid=(B, nt),
        in_specs=[col(0), col(1), col(2),
                  pl.BlockSpec((1, 8, D), lambda b, i: (b, 0, 0)),
                  pl.BlockSpec((8, D), lambda b, i: (0, 0))],
        out_specs=[pl.BlockSpec((tt, D), lambda b, i: (b * nt + i, 0)),
                   pl.BlockSpec((1, 8, D), lambda b, i: (b, 0, 0))],
        out_shape=[jax.ShapeDtypeStruct((B * T, D), out_dtype), jax.ShapeDtypeStruct((B, 8, D), F32)],
        scratch_shapes=[pltpu.VMEM((8, D), F32)],
        compiler_params=_cparams(("parallel", "arbitrary")),
        name="conv_gate",
    )(proj, proj, proj, prev8, w_conv)


def _cmlp_mix_kernel(u_ref, v_ref, g_ref, b_ref, ws_ref, bs_ref, o_ref, vo_ref):
    c = u_ref.shape[0]
    v = _ln(v_ref[...], g_ref[...], b_ref[...])
    vo_ref[...] = v
    vb = v.astype(BF16)
    dg = v.shape[1] // SG_GROUPS
    tril = lax.broadcasted_iota(jnp.int32, (c, c), 0) >= lax.broadcasted_iota(jnp.int32, (c, c), 1)
    for g in range(SG_GROUPS):
        ws = jnp.where(tril, ws_ref[g], 0.0).astype(BF16)
        mixed = _dot(ws, vb[:, g * dg:(g + 1) * dg]) + bs_ref[g]
        o_ref[:, g * dg:(g + 1) * dg] = (u_ref[:, g * dg:(g + 1) * dg] * mixed).astype(o_ref.dtype)


def cmlp_mix(proj, ln_g, ln_b, w_s, b_s, *, c, out_dtype):
    M = proj.shape[0]
    D = proj.shape[1] // 2
    const = lambda a: pl.BlockSpec(a.shape, lambda i: (0,) * a.ndim)
    ln_g = ln_g.reshape(1, D)
    ln_b = ln_b.reshape(1, D)
    return pl.pallas_call(
        _cmlp_mix_kernel,
        grid=(M // c,),
        in_specs=[pl.BlockSpec((c, D), lambda i: (i, 0)), pl.BlockSpec((c, D), lambda i: (i, 1)),
                  const(ln_g), const(ln_b), const(w_s), const(b_s)],
        out_specs=[pl.BlockSpec((c, D), lambda i: (i, 0)), pl.BlockSpec((c, D), lambda i: (i, 0))],
        out_shape=[jax.ShapeDtypeStruct((M, D), out_dtype), jax.ShapeDtypeStruct((M, D), F32)],
        compiler_params=_cparams(("parallel",)),
        name="cmlp_mix",
    )(proj, proj, ln_g, ln_b, w_s, b_s)


def _router_kernel(x_ref, w_ref, o_ref):
    logits = jnp.dot(x_ref[...], w_ref[...], preferred_element_type=F32, precision=lax.Precision.HIGHEST)
    lane = lax.broadcasted_iota(jnp.int32, logits.shape, 1)
    lane_f = lane.astype(F32)
    lg = jnp.where(lane < N_EXPERTS, logits, -jnp.inf)
    m1 = jnp.max(lg, axis=-1, keepdims=True)
    i1 = jnp.min(jnp.where(lg == m1, lane_f, float(LANES)), axis=-1, keepdims=True)
    lg2 = jnp.where(lane_f == i1, -jnp.inf, lg)
    m2 = jnp.max(lg2, axis=-1, keepdims=True)
    i2 = jnp.min(jnp.where(lg2 == m2, lane_f, float(LANES)), axis=-1, keepdims=True)
    e = jnp.exp(m2 - m1)
    den = 1.0 + e
    g1 = 1.0 / den
    g2 = e / den
    o_ref[...] = jnp.where(lane == 0, i1, jnp.where(lane == 1, i2,
                                                    jnp.where(lane == 2, g1, jnp.where(lane == 3, g2, 0.0))))


def router_top2(x, w_router, *, tm):
    M, K = x.shape
    w = jnp.pad(w_router, ((0, 0), (0, LANES - N_EXPERTS)))
    return pl.pallas_call(
        _router_kernel,
        grid=(M // tm,),
        in_specs=[pl.BlockSpec((tm, K), lambda i: (i, 0)), pl.BlockSpec((K, LANES), lambda i: (0, 0))],
        out_specs=pl.BlockSpec((tm, LANES), lambda i: (i, 0)),
        out_shape=jax.ShapeDtypeStruct((M, LANES), F32),
        compiler_params=_cparams(("parallel",)),
        name="moe_router",
    )(x, w)


def _gmm_swiglu_kernel(be_ref, nv_ref, new_ref, x_ref, wg_ref, wu_ref, o_ref, wgb_ref, wub_ref):
    r = pl.program_id(1)

    @pl.when(r < nv_ref[0])
    def _():
        @pl.when(new_ref[r] == 1)
        def _():
            wgb_ref[...] = wg_ref[0].astype(BF16)
            wub_ref[...] = wu_ref[0].astype(BF16)

        x = x_ref[...]
        g = _dot(x, wgb_ref[...])
        u = _dot(x, wub_ref[...])
        o_ref[...] = (jax.nn.silu(g) * u).astype(o_ref.dtype)


def gmm_swiglu(xs, w_gu, block_e, n_valid, new_expert, *, rb, tn):
    P, K = xs.shape
    F = w_gu.shape[2] // 2
    nf = F // tn
    row = lambda r, nv: jnp.minimum(r, nv[0] - 1)
    return pl.pallas_call(
        _gmm_swiglu_kernel,
        grid_spec=pltpu.PrefetchScalarGridSpec(
            num_scalar_prefetch=3,
            grid=(nf, P // rb),
            in_specs=[pl.BlockSpec((rb, K), lambda n, r, be, nv, nw: (row(r, nv), 0)),
                      pl.BlockSpec((1, K, tn), lambda n, r, be, nv, nw: (be[row(r, nv)], 0, n)),
                      pl.BlockSpec((1, K, tn), lambda n, r, be, nv, nw: (be[row(r, nv)], 0, n + nf))],
            out_specs=pl.BlockSpec((rb, tn), lambda n, r, be, nv, nw: (row(r, nv), n)),
            scratch_shapes=[pltpu.VMEM((K, tn), BF16), pltpu.VMEM((K, tn), BF16)]),
        out_shape=jax.ShapeDtypeStruct((P, F), BF16),
        compiler_params=_cparams(("arbitrary", "arbitrary")),
        name="moe_gmm_swiglu",
    )(block_e, n_valid, new_expert, xs, w_gu, w_gu)


_IT_FIRST, _IT_LAST, _IT_NEW_W, _IT_SKIP = 1, 2, 4, 8


def _gmm_down_kernel(blk_ref, k_ref, e_ref, slot_ref, oblk_ref, flag_ref, h_ref, w_ref, o_ref, wb_ref, acc_ref):
    w = pl.program_id(0)
    f = flag_ref[w]

    @pl.when((f & _IT_SKIP) == 0)
    def _():
        @pl.when((f & _IT_NEW_W) != 0)
        def _():
            wb_ref[...] = w_ref[0].astype(BF16)

        slot = slot_ref[w]
        part = _dot(h_ref[...], wb_ref[...])

        @pl.when((f & _IT_FIRST) != 0)
        def _():
            acc_ref[slot] = part

        @pl.when((f & _IT_FIRST) == 0)
        def _():
            acc_ref[slot] += part

        @pl.when((f & _IT_LAST) != 0)
        def _():
            o_ref[...] = acc_ref[slot]


def gmm_down(h, w_down, items, *, rb, tk, run):
    P, F = h.shape
    D = w_down.shape[2]
    it_blk, it_k, it_e, it_slot, it_oblk, it_flag = items
    n_items = it_blk.shape[0]
    return pl.pallas_call(
        _gmm_down_kernel,
        grid_spec=pltpu.PrefetchScalarGridSpec(
            num_scalar_prefetch=6,
            grid=(n_items,),
            in_specs=[pl.BlockSpec((rb, tk), lambda w, blk, k, e, s, ob, fl: (blk[w], k[w])),
                      pl.BlockSpec((1, tk, D), lambda w, blk, k, e, s, ob, fl: (e[w], k[w], 0))],
            out_specs=pl.BlockSpec((rb, D), lambda w, blk, k, e, s, ob, fl: (ob[w], 0)),
            scratch_shapes=[pltpu.VMEM((tk, D), BF16), pltpu.VMEM((run, rb, D), F32)]),
        out_shape=jax.ShapeDtypeStruct((P, D), F32),
        compiler_params=_cparams(("arbitrary",)),
        name="moe_gmm_down",
    )(it_blk, it_k, it_e, it_slot, it_oblk, it_flag, h, w_down)


def _moe_combine_ln_kernel(res_ref, y0_ref, y1_ref, r_ref, g_ref, b_ref, o_ref, ob_ref):
    r = r_ref[...]
    f = r[:, 2:3] * y0_ref[...] + r[:, 3:4] * y1_ref[...]
    y = _ln(DEEPNORM_ALPHA * res_ref[...] + f, g_ref[...], b_ref[...])
    o_ref[...] = y
    ob_ref[...] = y.astype(BF16)


def moe_combine_ln(res, y0, y1, routed, gain, bias, *, tm):
    M, N = res.shape
    row = pl.BlockSpec((tm, N), lambda i: (i, 0))
    vec = pl.BlockSpec((1, N), lambda i: (0, 0))
    return pl.pallas_call(
        _moe_combine_ln_kernel,
        grid=(M // tm,),
        in_specs=[row, row, row, pl.BlockSpec((tm, LANES), lambda i: (i, 0)), vec, vec],
        out_specs=[row, row],
        out_shape=[jax.ShapeDtypeStruct((M, N), F32), jax.ShapeDtypeStruct((M, N), BF16)],
        compiler_params=_cparams(("parallel",)),
        name="moe_combine_ln",
    )(res, y0, y1, routed, gain.reshape(1, N), bias.reshape(1, N))


def moe_dispatch(top_e, *, rb, run, nk):
    A = top_e.size
    flat_e = top_e.reshape(A)
    experts = jnp.arange(N_EXPERTS, dtype=jnp.int32)
    onehot = (flat_e[:, None] == experts[None, :]).astype(jnp.int32)
    pos_in_e = jnp.sum((jnp.cumsum(onehot, axis=0) - onehot) * onehot, axis=1)
    counts = jnp.sum(onehot, axis=0)
    nblk_e = (counts + rb - 1) // rb
    bend_e = jnp.cumsum(nblk_e)
    bstart_e = bend_e - nblk_e
    dest = (bstart_e[flat_e] * rb + pos_in_e).astype(jnp.int32)
    n_blocks = -(-(A + N_EXPERTS * (rb - 1)) // rb)
    row_tok = jnp.zeros((n_blocks * rb,), jnp.int32).at[dest].set(jnp.arange(A, dtype=jnp.int32) // TOP_K)
    b = jnp.arange(n_blocks, dtype=jnp.int32)
    block_e = jnp.minimum(jnp.sum((bend_e[None, :] <= b[:, None]).astype(jnp.int32), axis=1), N_EXPERTS - 1)
    n_valid = bend_e[-1]
    new_expert = ((b == 0) | (block_e != jnp.roll(block_e, 1))).astype(jnp.int32)
    idx_in_e = b - bstart_e[block_e]
    slot = idx_in_e % run
    first = b - slot
    n_run = jnp.minimum(run, nblk_e[block_e] - (idx_in_e - slot))
    k = jnp.arange(nk, dtype=jnp.int32)[None, :]
    n_items = nk * n_blocks
    w_idx = nk * first[:, None] + k * n_run[:, None] + slot[:, None]
    w_idx = jnp.where((b < n_valid)[:, None], w_idx, n_items).reshape(-1)
    flag = (jnp.where(k == 0, _IT_FIRST, 0) | jnp.where(k == nk - 1, _IT_LAST, 0)
            | jnp.where(slot[:, None] == 0, _IT_NEW_W, 0))
    full = lambda v: jnp.broadcast_to(v, (n_blocks, nk)).reshape(-1).astype(jnp.int32)
    cols = [full(b[:, None]), full(k), full(block_e[:, None]), full(slot[:, None]),
            full(jnp.where(k == nk - 1, b[:, None], first[:, None])), full(flag)]
    items = [jnp.zeros((n_items,), jnp.int32).at[w_idx].set(c, mode="drop") for c in cols]
    w = jnp.arange(n_items, dtype=jnp.int32)
    last = nk * n_valid - 1
    items = [jnp.where(w <= last, c, c[last]) for c in items]
    items[5] = jnp.where(w <= last, items[5], _IT_SKIP)
    return dest, row_tok, block_e, n_valid.reshape(1), new_expert, items


MOE_RB = 384
MOE_RUN = 6
MOE_TK = 1024


def moe_layer(xp, xpb, xs, xsb, w_router, w_gu, w_down, gain, bias):
    Mp, Ms = xp.shape[0], xs.shape[0]
    rp = router_top2(xp, w_router, tm=1024)
    rs = router_top2(xs, w_router, tm=Ms)
    top_e = jnp.concatenate([rp[:, 0:2], rs[:, 0:2]], axis=0).astype(jnp.int32)
    nk = w_down.shape[1] // MOE_TK
    dest, row_tok, block_e, n_valid, new_expert, items = moe_dispatch(top_e, rb=MOE_RB, run=MOE_RUN, nk=nk)
    x_rows = jnp.concatenate([xpb, xsb], axis=0)[row_tok]
    h = gmm_swiglu(x_rows, w_gu, block_e, n_valid, new_expert, rb=MOE_RB, tn=1024)
    y = gmm_down(h, w_down, items, rb=MOE_RB, tk=MOE_TK, run=MOE_RUN)
    dest = dest.reshape(Mp + Ms, TOP_K)
    dp, ds = dest[:Mp], dest[Mp:]
    xp, xpb = moe_combine_ln(xp, y[dp[:, 0]], y[dp[:, 1]], rp, gain, bias, tm=512)
    xs, xsb = moe_combine_ln(xs, y[ds[:, 0]], y[ds[:, 1]], rs, gain, bias, tm=Ms)
    return xp, xpb, xs, xsb


def _compress_weights(cmp_w1, cmp_pe, cmp_w2):
    s = CMP_STRIDE
    wcat = jnp.concatenate([cmp_w1[0, :s], cmp_w1[0, s:], cmp_w1[1, :s], cmp_w1[1, s:]], axis=-1)
    wcat = wcat.reshape(s // 2, 2 * HEAD_DIM, 4 * HEAD_DIM).astype(BF16)
    return wcat, compress_pe_term(cmp_w1, cmp_pe), cmp_w2.astype(BF16)


def _gates_by_group(gates, M):
    g = gates[:, :3 * N_HEADS].reshape(M, 3, KV_HEADS, HPG).transpose(2, 0, 1, 3).reshape(KV_HEADS, M, 3 * HPG)
    return jnp.pad(g, ((0, 0), (0, 0), (0, LANES - 3 * HPG)))


def _gate_weights(w_in):
    c0 = N_HEADS * HEAD_DIM + 6 * KV_WIDTH
    return jnp.pad(w_in[:, c0:], ((0, 0), (0, LANES - 3 * N_HEADS)))


def _sub_block_view(rows):
    return rows.reshape(-1, PAGE_SIZE // CMP_STRIDE, CMP_STRIDE, 4 * KV_HEADS, HEAD_DIM)


def nsa_prompt_mixer(xb, w_in, cmp_w, cos, sin, B, T, *, tm):
    M = B * T
    wcat, pe, w2 = cmp_w
    q, rows, win = nsa_proj(xb, w_in, cos, sin, tm=tm, q_dtype=BF16)
    gates = linear(xb, _gate_weights(w_in), tm=tm, tn=LANES, act="sigmoid")
    page_ids = jnp.arange(M // PAGE_SIZE, dtype=jnp.int32)
    cmp = compress(_sub_block_view(rows), page_ids, B, wcat, pe, w2)
    o = nsa_prompt_attend(q, cmp, rows.reshape(B, T, 4 * KV_WIDTH), win.reshape(B, T, 2 * KV_WIDTH),
                          _gates_by_group(gates, M), B, T)
    wn = min(WINDOW, T)
    return (o, rows.reshape(B, T, 4, KV_HEADS, HEAD_DIM),
            win.reshape(B, T, 2, KV_HEADS, HEAD_DIM)[:, T - wn:])


def nsa_sample_mixer(xb, w_in, cmp_w, cos, sin, cache, page_ids, win_buf, B, T):
    M = B * T
    past_len = page_ids.shape[0] // B * PAGE_SIZE
    wcat, pe, w2 = cmp_w
    q, rows, win = nsa_proj(xb, w_in, cos, sin, tm=M, q_dtype=F32)
    gates = linear(xb, _gate_weights(w_in), tm=M, tn=LANES, act="sigmoid")
    cmp = compress(_sub_block_view(cache), page_ids, B, wcat, pe, w2)
    pages = cache.reshape(-1, PAGE_SIZE, 4 * KV_HEADS, HEAD_DIM)
    rows3 = rows.reshape(B, T, 4 * KV_WIDTH)
    new_rows = jnp.pad(rows3[:, :, 2 * KV_WIDTH:], ((0, 0), (0, LANES - T), (0, 0)))
    nbuf = win_buf.shape[1]
    band = jnp.concatenate([win_buf.reshape(B, nbuf, 2 * KV_WIDTH), win.reshape(B, T, 2 * KV_WIDTH)], axis=1)
    band_pad = jnp.pad(band, ((0, 0), (0, WINDOW + LANES - band.shape[1]), (0, 0)))
    o = nsa_sample_attend(q.reshape(B, T, -1), cmp, pages, page_ids, new_rows, band_pad,
                          gates.reshape(B, T, LANES), past_len)
    return (o.reshape(M, -1), rows.reshape(B, T, 4, KV_HEADS, HEAD_DIM),
            band[:, band.shape[1] - nbuf:].reshape(B, nbuf, 2, KV_HEADS, HEAD_DIM))


def conv_mixer(xb, prev, w_in, w_conv, B, T, *, tm, tt):
    keep = CONV_WIDTH - 1
    proj = linear(xb, w_in, tm=tm, tn=512)
    prev8 = jnp.pad(prev, ((0, 0), (8 - keep, 0), (0, 0)))
    w8 = jnp.pad(w_conv, ((0, 8 - CONV_WIDTH), (0, 0)))
    gated, st = conv_gate(proj, prev8, w8, B, T, tt=tt, out_dtype=BF16 if tt % 16 == 0 else F32)
    return gated, st[:, 8 - keep:]


def chunk_mlp_mixer(xb, w_in, ln_g, ln_b, w_s, b_s, B, T, *, tm):
    D = w_in.shape[1] // 2
    c = min(T, CHUNK)
    proj = linear(xb, w_in, tm=tm, tn=512, act="gelu")
    cp = max(c, 16)
    if cp != c:
        proj = jnp.pad(proj.reshape(B * T // c, c, 2 * D), ((0, 0), (0, cp - c), (0, 0))).reshape(-1, 2 * D)
    ws = jnp.pad(w_s[:, :c, :c], ((0, 0), (0, cp - c), (0, cp - c)))
    bs = jnp.pad(b_s[:, :c, None], ((0, 0), (0, cp - c), (0, 0)))
    mixed, v = cmlp_mix(proj, ln_g, ln_b, ws, bs, c=cp, out_dtype=BF16 if c == cp else F32)
    if cp != c:
        mixed = mixed.reshape(-1, cp, D)[:, :c].reshape(B * T, D)
        v = v.reshape(-1, cp, D)[:, :c].reshape(B * T, D)
    return mixed, v.reshape(B, T, D)[:, T - c:]


def kernel(x_prompt, x_sample, cache_nsa_kv, state_nsa_win, state_conv, page_table, ln_gain, ln_bias, nsa_w_in, nsa_cmp_w1, nsa_cmp_pe, nsa_cmp_w2, nsa_w_out, conv_w_in, conv_w, conv_w_out, cmlp_w_in, cmlp_ln_gain, cmlp_ln_bias, cmlp_w_s, cmlp_b_s, cmlp_w_out, ffn_w_gu, ffn_w_down, moe_router, moe_w_gu, moe_w_down):
    B, T, D = x_prompt.shape
    SB, ST, _ = x_sample.shape
    Mp, Ms = B * T, SB * ST
    n_pool = cache_nsa_kv.shape[1]
    n_pages = page_table.shape[1]
    past_len = n_pages * PAGE_SIZE
    tm_p, tm_s = 1024, Ms

    xp = x_prompt.reshape(Mp, D)
    xs = x_sample.reshape(Ms, D)
    xpb, xsb = xp, xs

    pos_p = jnp.tile(jnp.arange(T, dtype=jnp.int32), B)
    pos_s = jnp.tile(past_len + jnp.arange(ST, dtype=jnp.int32), SB)
    cos_p, sin_p = rope_tables(pos_p)
    cos_s, sin_s = rope_tables(pos_s)

    rows_p, rows_s, win_p, win_s = [], [], [], []
    conv_p, conv_s, v_p, v_s = [], [], [], []
    for i in range(DEPTH):
        m = i // 3
        kind = i % 3
        g0, b0 = ln_gain[i, 0], ln_bias[i, 0]
        if kind == 0:
            w = (nsa_w_in[m], _compress_weights(nsa_cmp_w1[m], nsa_cmp_pe[m], nsa_cmp_w2[m]))
            o, rows, win = nsa_prompt_mixer(xpb, *w, cos_p, sin_p, B, T, tm=tm_p)
            xp, xpb = linear_res_ln(o, nsa_w_out[m], xp, g0, b0, tm=512, tk=512)
            rows_p.append(rows)
            win_p.append(win)
            page_ids = (m * n_pool + page_table).reshape(-1).astype(jnp.int32)
            o, rows, win = nsa_sample_mixer(xsb, *w, cos_s, sin_s, cache_nsa_kv, page_ids, state_nsa_win[m],
                                            SB, ST)
            xs, xsb = linear_res_ln(o, nsa_w_out[m], xs, g0, b0, tm=tm_s, tk=512)
            rows_s.append(rows)
            win_s.append(win)
        elif kind == 1:
            w = (conv_w_in[m], conv_w[m])
            gated, st = conv_mixer(xpb, jnp.zeros((B, CONV_WIDTH - 1, D), F32), *w, B, T, tm=tm_p, tt=256)
            xp, xpb = linear_res_ln(gated, conv_w_out[m], xp, g0, b0, tm=512, tk=512)
            conv_p.append(st)
            gated, st = conv_mixer(xsb, state_conv[m], *w, SB, ST, tm=tm_s, tt=ST)
            xs, xsb = linear_res_ln(gated, conv_w_out[m], xs, g0, b0, tm=tm_s, tk=512)
            conv_s.append(st)
        else:
            w = (cmlp_w_in[m], cmlp_ln_gain[m], cmlp_ln_bias[m], cmlp_w_s[m], cmlp_b_s[m])
            mixed, v = chunk_mlp_mixer(xpb, *w, B, T, tm=tm_p)
            xp, xpb = linear_res_ln(mixed, cmlp_w_out[m], xp, g0, b0, tm=512, tk=512)
            v_p.append(v)
            mixed, v = chunk_mlp_mixer(xsb, *w, SB, ST, tm=tm_s)
            xs, xsb = linear_res_ln(mixed, cmlp_w_out[m], xs, g0, b0, tm=tm_s, tk=512)
            v_s.append(v)
        g1, b1 = ln_gain[i, 1], ln_bias[i, 1]
        f = i // 2
        if i % 2 == 0:
            h = linear_swiglu(xpb, ffn_w_gu[f], tm=tm_p, tn=512)
            xp, xpb = linear_res_ln(h, ffn_w_down[f], xp, g1, b1, tm=512, tk=512)
            h = linear_swiglu(xsb, ffn_w_gu[f], tm=tm_s, tn=512)
            xs, xsb = linear_res_ln(h, ffn_w_down[f], xs, g1, b1, tm=tm_s, tk=512)
        else:
            xp, xpb, xs, xsb = moe_layer(xp, xpb, xs, xsb, moe_router[f], moe_w_gu[f], moe_w_down[f], g1, b1)
    return (xp.reshape(B, T, D), xs.reshape(SB, ST, D),
            jnp.stack(rows_p), jnp.stack(rows_s), jnp.stack(win_p), jnp.stack(win_s),
            jnp.stack(conv_p), jnp.stack(conv_s), jnp.stack(v_p), jnp.stack(v_s))
```

```python
import functools

import jax
import jax.numpy as jnp
from jax import lax
from jax.experimental import pallas as pl
from jax.experimental.pallas import tpu as pltpu

D_MODEL = 2048
DEPTH = 4
PAGE_SIZE = 128
N_HEADS = 16
HEAD_DIM = 128
KV_HEADS = 4
HPG = N_HEADS // KV_HEADS
KV_WIDTH = KV_HEADS * HEAD_DIM
CMP_BLOCK = 32
CMP_STRIDE = 16
SEL_BLOCK = 64
SEL_TOPK = 16
WINDOW = 512
ROPE_THETA = 10000.0
FORCE_BONUS = 1e4
CONV_WIDTH = 3
CHUNK = 128
SG_GROUPS = 8
N_EXPERTS = 8
TOP_K = 2
LN_EPS = 1e-5
DEEPNORM_ALPHA = (2 * DEPTH) ** 0.25

LANES = 128
PAGES_PER_STEP = 8
NEG = -1e30
Q_SCALE = HEAD_DIM ** -0.5 * 1.4426950408889634
VMEM_LIMIT = 56 * 1024 * 1024

F32 = jnp.float32
BF16 = jnp.bfloat16


def _cparams(sem):
    return pltpu.CompilerParams(dimension_semantics=sem, vmem_limit_bytes=VMEM_LIMIT)


def _ln(y, g, b):
    mu = jnp.mean(y, axis=-1, keepdims=True)
    d = y - mu
    var = jnp.mean(d * d, axis=-1, keepdims=True)
    return d * lax.rsqrt(var + LN_EPS) * g + b


def _dot(a, b):
    return jnp.dot(a, b, preferred_element_type=F32)


def _dot_nt(a, b, precision=None):
    return lax.dot_general(a, b, (((1,), (1,)), ((), ())), preferred_element_type=F32,
                           precision=precision)


def _stacked(w):
    return w if w.ndim == 3 else w[None]


def _linear_kernel(x_ref, w_ref, o_ref, *, act):
    y = _dot(x_ref[...].astype(BF16), w_ref[0].astype(BF16))
    if act == "gelu":
        y = jax.nn.gelu(y)
    elif act == "sigmoid":
        y = jax.nn.sigmoid(y)
    o_ref[...] = y.astype(o_ref.dtype)


def linear(x, w, *, tm, tn, layer=0, act=None, out_dtype=F32):
    w = _stacked(w)
    M, K = x.shape
    N = w.shape[2]
    return pl.pallas_call(
        functools.partial(_linear_kernel, act=act),
        grid=(M // tm, N // tn),
        in_specs=[pl.BlockSpec((tm, K), lambda i, j: (i, 0)),
                  pl.BlockSpec((1, K, tn), lambda i, j: (layer, 0, j))],
        out_specs=pl.BlockSpec((tm, tn), lambda i, j: (i, j)),
        out_shape=jax.ShapeDtypeStruct((M, N), out_dtype),
        compiler_params=_cparams(("parallel", "arbitrary")),
        name="linear",
    )(x, w)


def _swiglu_kernel(x_ref, wg_ref, wu_ref, o_ref):
    x = x_ref[...].astype(BF16)
    g = _dot(x, wg_ref[0].astype(BF16))
    u = _dot(x, wu_ref[0].astype(BF16))
    o_ref[...] = (jax.nn.silu(g) * u).astype(o_ref.dtype)


def linear_swiglu(x, w_gu, *, tm, tn, layer=0):
    w_gu = _stacked(w_gu)
    M, K = x.shape
    F = w_gu.shape[2] // 2
    nf = F // tn
    return pl.pallas_call(
        _swiglu_kernel,
        grid=(M // tm, nf),
        in_specs=[pl.BlockSpec((tm, K), lambda i, j: (i, 0)),
                  pl.BlockSpec((1, K, tn), lambda i, j: (layer, 0, j)),
                  pl.BlockSpec((1, K, tn), lambda i, j: (layer, 0, j + nf))],
        out_specs=pl.BlockSpec((tm, tn), lambda i, j: (i, j)),
        out_shape=jax.ShapeDtypeStruct((M, F), BF16),
        compiler_params=_cparams(("parallel", "arbitrary")),
        name="linear_swiglu",
    )(x, w_gu, w_gu)


def _linear_res_ln_kernel(h_ref, w_ref, res_ref, g_ref, b_ref, o_ref, ob_ref, *acc):
    def finish(f):
        y = _ln(DEEPNORM_ALPHA * res_ref[...] + f, g_ref[...], b_ref[...])
        o_ref[...] = y
        ob_ref[...] = y.astype(BF16)

    part = _dot(h_ref[...].astype(BF16), w_ref[0].astype(BF16))
    if not acc:
        finish(part)
        return
    acc_ref, = acc
    k = pl.program_id(1)

    @pl.when(k == 0)
    def _():
        acc_ref[...] = part

    @pl.when(k > 0)
    def _():
        acc_ref[...] += part

    @pl.when(k == pl.num_programs(1) - 1)
    def _():
        finish(acc_ref[...])


def linear_res_ln(h, w, res, gain, bias, *, tm, tk, layer=0):
    w = _stacked(w)
    M, K = h.shape
    N = w.shape[2]
    return pl.pallas_call(
        _linear_res_ln_kernel,
        grid=(M // tm, K // tk),
        in_specs=[pl.BlockSpec((tm, tk), lambda i, k: (i, k)),
                  pl.BlockSpec((1, tk, N), lambda i, k: (layer, k, 0)),
                  pl.BlockSpec((tm, N), lambda i, k: (i, 0)),
                  pl.BlockSpec((1, N), lambda i, k: (0, 0)),
                  pl.BlockSpec((1, N), lambda i, k: (0, 0))],
        out_specs=[pl.BlockSpec((tm, N), lambda i, k: (i, 0)),
                   pl.BlockSpec((tm, N), lambda i, k: (i, 0))],
        out_shape=[jax.ShapeDtypeStruct((M, N), F32), jax.ShapeDtypeStruct((M, N), BF16)],
        scratch_shapes=[pltpu.VMEM((tm, N), F32)] if K != tk else [],
        compiler_params=_cparams(("parallel", "arbitrary")),
        name="linear_res_ln",
    )(h, w, res, gain.reshape(1, N), bias.reshape(1, N))


def _rope_tile(y, cos, sin):
    outs = []
    for c in range(y.shape[1] // HEAD_DIM):
        yc = y[:, c * HEAD_DIM:(c + 1) * HEAD_DIM]
        outs.append(yc * cos + pltpu.roll(yc, HEAD_DIM // 2, axis=1) * sin)
    return jnp.concatenate(outs, axis=1)


def _nsa_proj_kernel(x_ref, w_ref, cos_ref, sin_ref, q_ref, rows_ref, win_ref):
    j = pl.program_id(1)
    y = _dot(x_ref[...].astype(BF16), w_ref[0].astype(BF16))
    cos = cos_ref[...]
    sin = sin_ref[...]

    @pl.when(j < 4)
    def _():
        q_ref[...] = (_rope_tile(y, cos, sin) * Q_SCALE).astype(q_ref.dtype)

    @pl.when((j == 4) | (j == 6))
    def _():
        rows_ref[...] = _rope_tile(y, cos, sin)

    @pl.when((j == 5) | (j == 7))
    def _():
        rows_ref[...] = y

    @pl.when(j == 8)
    def _():
        win_ref[...] = _rope_tile(y, cos, sin)

    @pl.when(j == 9)
    def _():
        win_ref[...] = y


def nsa_proj(x, w_in, cos, sin, *, tm, q_dtype, layer=0):
    w_in = _stacked(w_in)
    M, K = x.shape
    tn = KV_WIDTH
    qd = N_HEADS * HEAD_DIM
    return pl.pallas_call(
        _nsa_proj_kernel,
        grid=(M // tm, 10),
        in_specs=[pl.BlockSpec((tm, K), lambda i, j: (i, 0)),
                  pl.BlockSpec((1, K, tn), lambda i, j: (layer, 0, j)),
                  pl.BlockSpec((tm, HEAD_DIM), lambda i, j: (i, 0)),
                  pl.BlockSpec((tm, HEAD_DIM), lambda i, j: (i, 0))],
        out_specs=[pl.BlockSpec((tm, tn), lambda i, j: (i, jnp.minimum(j, 3))),
                   pl.BlockSpec((tm, tn), lambda i, j: (i, jnp.clip(j - 4, 0, 3))),
                   pl.BlockSpec((tm, tn), lambda i, j: (i, jnp.clip(j - 8, 0, 1)))],
        out_shape=[jax.ShapeDtypeStruct((M, qd), q_dtype),
                   jax.ShapeDtypeStruct((M, 4 * KV_WIDTH), F32),
                   jax.ShapeDtypeStruct((M, 2 * KV_WIDTH), F32)],
        compiler_params=_cparams(("parallel", "arbitrary")),
        name="nsa_proj",
    )(x, w_in, cos, sin)


def rope_tables(pos):
    half = HEAD_DIM // 2
    inv = ROPE_THETA ** (-jnp.arange(half, dtype=F32) / half)
    ang = pos.astype(F32)[:, None] * inv[None, :]
    cos = jnp.cos(ang)
    sin = jnp.sin(ang)
    return jnp.concatenate([cos, cos], -1), jnp.concatenate([-sin, sin], -1)


def _compress_kernel(pt_ref, *refs):
    npg = PAGES_PER_STEP
    page_refs = refs[:npg]
    wcat_ref, pe_ref, w2_ref, o_ref, carry_ref = refs[npg:]
    j = pl.program_id(1)
    nch = 2 * KV_HEADS
    nsub = PAGE_SIZE // CMP_STRIDE
    R = npg * nsub * nch

    @pl.when(j == 0)
    def _():
        carry_ref[...] = jnp.zeros_like(carry_ref)

    def rows_at(l):
        return jnp.concatenate([page_refs[p][0, :, l].reshape(nsub * nch, HEAD_DIM) for p in range(npg)],
                               axis=0).astype(BF16)

    acc = jnp.zeros((R, 4 * HEAD_DIM), F32)
    for l2 in range(CMP_STRIDE // 2):
        x = jnp.concatenate([rows_at(2 * l2), rows_at(2 * l2 + 1)], axis=1)
        acc = acc + _dot(x, wcat_ref[l2])
    outs = []
    for t in range(2):
        a = acc[:, 2 * t * HEAD_DIM:(2 * t + 1) * HEAD_DIM]
        b = acc[:, (2 * t + 1) * HEAD_DIM:2 * (t + 1) * HEAD_DIM]
        a_shift = jnp.concatenate([carry_ref[t], a[:R - nch]], axis=0)
        carry_ref[t] = a[R - nch:]
        h = a_shift + b + pe_ref[t:t + 1, :]
        outs.append(_dot(jax.nn.gelu(h).astype(BF16), w2_ref[t]))
    is_k = lax.broadcasted_iota(jnp.int32, (R, HEAD_DIM), 0) % nch < KV_HEADS
    o_ref[0] = jnp.where(is_k, outs[0], outs[1])


def _compress_pe_kernel(w1_ref, pe_ref, o_ref):
    for t in range(2):
        acc = jnp.zeros((HEAD_DIM, HEAD_DIM), F32)
        for l in range(CMP_BLOCK):
            acc = acc + w1_ref[t, l] * pe_ref[t, l]
        o_ref[t:t + 1, :] = jnp.sum(acc, axis=0, keepdims=True)


def compress_pe_term(cmp_w1, cmp_pe):
    return pl.pallas_call(
        _compress_pe_kernel,
        out_shape=jax.ShapeDtypeStruct((2, HEAD_DIM), F32),
        compiler_params=_cparams(()),
        name="nsa_compress_pe",
    )(cmp_w1, cmp_pe[..., None])


def compress(pages, page_ids, n_batch, wcat, pe, w2):
    npg = PAGES_PER_STEP
    nch = 2 * KV_HEADS
    nsub = PAGE_SIZE // CMP_STRIDE
    ppb = page_ids.shape[0] // n_batch
    steps = ppb // npg
    R = npg * nsub * nch

    def page_spec(p):
        return pl.BlockSpec((1, nsub, CMP_STRIDE, nch, HEAD_DIM),
                            lambda b, j, pt: (pt[b * ppb + j * npg + p], 0, 0, 0, 0))

    const = lambda shape: pl.BlockSpec(shape, lambda b, j, pt: (0,) * len(shape))
    out = pl.pallas_call(
        _compress_kernel,
        grid_spec=pltpu.PrefetchScalarGridSpec(
            num_scalar_prefetch=1,
            grid=(n_batch, steps),
            in_specs=[page_spec(p) for p in range(npg)]
            + [const(wcat.shape), const(pe.shape), const(w2.shape)],
            out_specs=pl.BlockSpec((1, R, HEAD_DIM), lambda b, j, pt: (b, j, 0)),
            scratch_shapes=[pltpu.VMEM((2, nch, HEAD_DIM), F32)]),
        out_shape=jax.ShapeDtypeStruct((n_batch, steps * R, HEAD_DIM), F32),
        compiler_params=_cparams(("parallel", "arbitrary")),
        name="nsa_compress",
    )(page_ids, *([pages] * npg), wcat, pe, w2)
    return out.reshape(n_batch, steps * npg * nsub, nch * HEAD_DIM)


def _softmax_rows(s, mask):
    sm = jnp.where(mask, s, NEG)
    m = jnp.max(sm, axis=-1, keepdims=True)
    p = jnp.where(mask, jnp.exp2(sm - m), 0.0)
    return p * (1.0 / jnp.maximum(jnp.sum(p, axis=-1, keepdims=True), 1e-30))


def _online_step(s, mask, v, m, l, acc):
    sm = jnp.where(mask, s, NEG)
    m_new = jnp.maximum(m, jnp.max(sm, axis=-1, keepdims=True))
    p = jnp.where(mask, jnp.exp2(sm - m_new), 0.0)
    alpha = jnp.exp2(m - m_new)
    l = alpha * l + jnp.sum(p, axis=-1, keepdims=True)
    acc = alpha * acc + _dot(p.astype(BF16), v)
    return m_new, l, acc


def _mask_bias(mask):
    return jnp.where(mask, 0.0, NEG).astype(F32)


def _online_step_biased(s, bias, v, m, l, acc):
    R, C = s.shape
    T = bias.shape[0]
    sm = (s.reshape(R // T, T, C) + bias[None]).reshape(R, C)
    m_new = jnp.maximum(m, jnp.max(sm, axis=-1, keepdims=True))
    p = jnp.exp2(sm - m_new)
    alpha = jnp.exp2(m - m_new)
    l = alpha * l + jnp.sum(p, axis=-1, keepdims=True)
    acc = alpha * acc + _dot(p.astype(BF16), v)
    return m_new, l, acc


def _pool_matrix(n_rows, n_cols, transposed):
    shape = (n_cols, n_rows) if transposed else (n_rows, n_cols)
    n_ax, j_ax = (1, 0) if transposed else (0, 1)
    d = lax.broadcasted_iota(jnp.int32, shape, n_ax) - 4 * lax.broadcasted_iota(jnp.int32, shape, j_ax)
    return jnp.where((d == 0) | (d == 4), 1.0, jnp.where((d > 0) & (d < 4), 2.0, 0.0)).astype(F32)


TQ = 128
SEL_CHUNK = 512


def _nsa_prompt_kernel(q_ref, kc_ref, vc_ref, ks_ref, vs_ref, kw_ref, vw_ref, gate_ref, o_ref):
    qi = pl.program_id(2)
    t0 = qi * TQ
    R = HPG * TQ
    q = q_ref[...]
    qs = jnp.concatenate([q[:, h * HEAD_DIM:(h + 1) * HEAD_DIM] for h in range(HPG)], axis=0)
    t_row = t0 + lax.broadcasted_iota(jnp.int32, (TQ, 1), 0)
    t_rows = jnp.concatenate([t_row] * HPG, axis=0)

    nc = kc_ref.shape[1]
    s = _dot_nt(qs, kc_ref[0].astype(BF16))
    n_idx = lax.broadcasted_iota(jnp.int32, (1, nc), 1)
    cmask = (n_idx >= 1) & (CMP_STRIDE * n_idx + (CMP_STRIDE - 1) <= t_rows)
    p_cmp = _softmax_rows(s, cmask)
    o_cmp = _dot(p_cmp.astype(BF16), vc_ref[0].astype(BF16))
    pg = p_cmp[0:TQ]
    for h in range(1, HPG):
        pg = pg + p_cmp[h * TQ:(h + 1) * TQ]

    ns = ks_ref.shape[1] // SEL_BLOCK
    p_slc = _dot_nt(_pool_matrix(nc, ns, True), pg, precision=lax.Precision.HIGHEST)
    blk = lax.broadcasted_iota(jnp.int32, (ns, TQ), 0)
    cur = (t0 + lax.broadcasted_iota(jnp.int32, (ns, TQ), 1)) // SEL_BLOCK
    avail = blk <= cur
    forced = (blk == 0) | (blk == cur) | (blk == cur - 1)
    score = jnp.where(avail, p_slc + jnp.where(forced, FORCE_BONUS, 0.0), -jnp.inf)
    rank = jnp.zeros((ns, TQ), jnp.int32)
    for jj in range(ns):
        row = score[jj:jj + 1, :]
        beats = (row > score) | ((row == score) & (blk > jj))
        rank = rank + beats.astype(jnp.int32)
    sel_t = jnp.where((rank < SEL_TOPK) & avail, 1.0, 0.0).astype(F32)
    sel_t = jnp.concatenate([sel_t, jnp.zeros((LANES - ns, TQ), F32)], axis=0)
    sel = sel_t.T.astype(BF16)

    def sel_body(c, carry):
        k0 = pl.multiple_of(c * SEL_CHUNK, SEL_CHUNK)
        key = k0 + lax.broadcasted_iota(jnp.int32, (LANES, SEL_CHUNK), 1)
        expand = (lax.broadcasted_iota(jnp.int32, (LANES, SEL_CHUNK), 0) == key // SEL_BLOCK)
        picked = _dot(sel, expand.astype(BF16)) > 0.5
        kpos = k0 + lax.broadcasted_iota(jnp.int32, (1, SEL_CHUNK), 1)
        bias = _mask_bias(picked & (kpos <= t_row))
        k = ks_ref[0, pl.ds(k0, SEL_CHUNK), :].astype(BF16)
        v = vs_ref[0, pl.ds(k0, SEL_CHUNK), :].astype(BF16)
        return _online_step_biased(_dot_nt(qs, k), bias, v, *carry)

    init = (jnp.full((R, 1), NEG, F32), jnp.zeros((R, 1), F32), jnp.zeros((R, HEAD_DIM), F32))
    n_chunks = (t0 + TQ + SEL_CHUNK - 1) // SEL_CHUNK
    _, l, acc = lax.fori_loop(0, n_chunks, sel_body, init)
    o_sel = acc * (1.0 / l)

    band = WINDOW + TQ
    start = pl.multiple_of(jnp.maximum(t0 - WINDOW, 0), TQ)
    kpos = start + lax.broadcasted_iota(jnp.int32, (1, band), 1)
    bias = _mask_bias((kpos <= t_row) & (kpos > t_row - WINDOW))
    s = _dot_nt(qs, kw_ref[0, pl.ds(start, band), :].astype(BF16))
    _, l, acc = _online_step_biased(s, bias, vw_ref[0, pl.ds(start, band), :].astype(BF16), *init)
    o_win = acc * (1.0 / l)

    gates = gate_ref[0]
    for h in range(HPG):
        rows = slice(h * TQ, (h + 1) * TQ)
        o = (gates[:, h:h + 1] * o_cmp[rows] + gates[:, HPG + h:HPG + h + 1] * o_sel[rows]
             + gates[:, 2 * HPG + h:2 * HPG + h + 1] * o_win[rows])
        o_ref[:, h * HEAD_DIM:(h + 1) * HEAD_DIM] = o.astype(o_ref.dtype)


def nsa_prompt_attend(q, cmp, rows, win, gates_g, B, T):
    nq = T // TQ
    kv = lambda col0: pl.BlockSpec((1, T, HEAD_DIM), lambda b, g, i: (b, 0, col0 + g))
    cmp_spec = lambda col0: pl.BlockSpec((1, cmp.shape[1], HEAD_DIM), lambda b, g, i: (b, 0, col0 + g))
    return pl.pallas_call(
        _nsa_prompt_kernel,
        grid=(B, KV_HEADS, nq),
        in_specs=[pl.BlockSpec((TQ, HPG * HEAD_DIM), lambda b, g, i: (b * nq + i, g)),
                  cmp_spec(0), cmp_spec(KV_HEADS),
                  kv(2 * KV_HEADS), kv(3 * KV_HEADS), kv(0), kv(KV_HEADS),
                  pl.BlockSpec((1, TQ, LANES), lambda b, g, i: (g, b * nq + i, 0))],
        out_specs=pl.BlockSpec((TQ, HPG * HEAD_DIM), lambda b, g, i: (b * nq + i, g)),
        out_shape=jax.ShapeDtypeStruct((B * T, N_HEADS * HEAD_DIM), BF16),
        compiler_params=_cparams(("parallel", "parallel", "arbitrary")),
        name="nsa_prompt_attend",
    )(q, cmp, cmp, rows, rows, win, win, gates_g)


def _nsa_sample_select_kernel(q_ref, cmp_ref, ocmp_ref, sel_ref, *, past_len, ns_pad):
    T = q_ref.shape[1]
    R = HPG * T
    nc = cmp_ref.shape[1]
    q = q_ref[0]
    t_pos = past_len + lax.broadcasted_iota(jnp.int32, (T, 1), 0)
    t_rows = jnp.concatenate([t_pos] * HPG, axis=0)
    n_idx = lax.broadcasted_iota(jnp.int32, (1, nc), 1)
    cmask = (n_idx >= 1) & (CMP_STRIDE * n_idx + (CMP_STRIDE - 1) <= t_rows)
    pool = _pool_matrix(nc, ns_pad, False)
    pgs = []
    for g in range(KV_HEADS):
        qs = jnp.concatenate(
            [q[:, (g * HPG + h) * HEAD_DIM:(g * HPG + h + 1) * HEAD_DIM] for h in range(HPG)],
            axis=0).astype(BF16)
        kc = cmp_ref[0, :, g * HEAD_DIM:(g + 1) * HEAD_DIM].astype(BF16)
        vc = cmp_ref[0, :, (KV_HEADS + g) * HEAD_DIM:(KV_HEADS + g + 1) * HEAD_DIM].astype(BF16)
        p = _softmax_rows(_dot_nt(qs, kc), cmask)
        ocmp_ref[0, g] = _dot(p.astype(BF16), vc)
        pg = p[0:T]
        for h in range(1, HPG):
            pg = pg + p[h * T:(h + 1) * T]
        pgs.append(pg)
    pg = jnp.concatenate(pgs, axis=0)
    p_slc = jnp.dot(pg, pool, preferred_element_type=F32, precision=lax.Precision.HIGHEST)
    blk = lax.broadcasted_iota(jnp.int32, (KV_HEADS * T, ns_pad), 1)
    cur = jnp.concatenate([t_pos] * KV_HEADS, axis=0) // SEL_BLOCK
    avail = blk <= cur
    forced = (blk == 0) | (blk == cur) | (blk == cur - 1)
    score = jnp.where(avail, p_slc + jnp.where(forced, FORCE_BONUS, 0.0), -jnp.inf)
    sel = jnp.zeros(score.shape, F32)
    blk_f = blk.astype(F32)
    for _ in range(SEL_TOPK):
        m = jnp.max(score, axis=-1, keepdims=True)
        first = jnp.min(jnp.where(score == m, blk_f, float(ns_pad)), axis=-1, keepdims=True)
        pick = blk_f == first
        sel = jnp.where(pick & (m > -jnp.inf), 1.0, sel)
        score = jnp.where(pick, -jnp.inf, score)
    sel_ref[0] = sel


def _nsa_sample_attend_kernel(pt_ref, *refs, past_len, ns_pad):
    npg = PAGES_PER_STEP
    page_refs = refs[:npg]
    (q_ref, sel_ref, new_ref, band_ref, ocmp_ref, gate_ref, o_ref, m_ref, l_ref, acc_ref) = refs[npg:]
    c = pl.program_id(1)
    T = q_ref.shape[1]
    R = HPG * T
    chunk = npg * PAGE_SIZE
    q = q_ref[0]
    t_pos = past_len + lax.broadcasted_iota(jnp.int32, (T, 1), 0)
    t_rows = jnp.concatenate([t_pos] * HPG, axis=0)

    @pl.when(c == 0)
    def _():
        m_ref[...] = jnp.full(m_ref.shape, NEG, F32)
        l_ref[...] = jnp.zeros_like(l_ref)
        acc_ref[...] = jnp.zeros_like(acc_ref)

    def qs_of(g):
        return jnp.concatenate(
            [q[:, (g * HPG + h) * HEAD_DIM:(g * HPG + h + 1) * HEAD_DIM] for h in range(HPG)],
            axis=0).astype(BF16)

    def picked_masks(first_key, n_keys):
        key = first_key + lax.broadcasted_iota(jnp.int32, (ns_pad, n_keys), 1)
        expand = lax.broadcasted_iota(jnp.int32, (ns_pad, n_keys), 0) == key // SEL_BLOCK
        hit = _dot(sel_ref[0].astype(BF16), expand.astype(BF16))
        kpos = first_key + lax.broadcasted_iota(jnp.int32, (1, n_keys), 1)
        out = []
        for g in range(KV_HEADS):
            m1 = (hit[g * T:(g + 1) * T] > 0.5) & (kpos <= t_pos)
            out.append(jnp.concatenate([m1] * HPG, axis=0))
        return out

    def update(g, s, mask, v):
        m, l, acc = _online_step(s, mask, v, m_ref[g][:, 0:1], l_ref[g][:, 0:1], acc_ref[g])
        m_ref[g] = jnp.broadcast_to(m, (R, LANES))
        l_ref[g] = jnp.broadcast_to(l, (R, LANES))
        acc_ref[g] = acc

    masks = picked_masks(c * chunk, chunk)
    for g in range(KV_HEADS):
        qs = qs_of(g)
        k = jnp.concatenate([page_refs[p][0, :, g, :] for p in range(npg)], axis=0).astype(BF16)
        v = jnp.concatenate([page_refs[p][0, :, KV_HEADS + g, :] for p in range(npg)], axis=0).astype(BF16)
        update(g, _dot_nt(qs, k), masks[g], v)

    @pl.when(c == pl.num_programs(1) - 1)
    def _():
        n_new = new_ref.shape[1]
        n_band = band_ref.shape[1]
        gates = gate_ref[0]
        new_masks = picked_masks(past_len, n_new)
        for g in range(KV_HEADS):
            qs = qs_of(g)
            k = new_ref[0, :, g * HEAD_DIM:(g + 1) * HEAD_DIM].astype(BF16)
            v = new_ref[0, :, KV_WIDTH + g * HEAD_DIM:KV_WIDTH + (g + 1) * HEAD_DIM].astype(BF16)
            update(g, _dot_nt(qs, k), new_masks[g], v)
            o_sel = acc_ref[g] * (1.0 / jnp.maximum(l_ref[g][:, 0:1], 1e-30))
            kpos = (past_len - WINDOW) + lax.broadcasted_iota(jnp.int32, (1, n_band), 1)
            wmask = (kpos <= t_rows) & (kpos > t_rows - WINDOW)
            kw = band_ref[0, :, g * HEAD_DIM:(g + 1) * HEAD_DIM].astype(BF16)
            vw = band_ref[0, :, KV_WIDTH + g * HEAD_DIM:KV_WIDTH + (g + 1) * HEAD_DIM].astype(BF16)
            p_win = _softmax_rows(_dot_nt(qs, kw), wmask)
            o_win = _dot(p_win.astype(BF16), vw)
            o_cmp = ocmp_ref[0, g]
            for h in range(HPG):
                rows = slice(h * T, (h + 1) * T)
                col = g * HPG + h
                o = (gates[:, col:col + 1] * o_cmp[rows]
                     + gates[:, N_HEADS + col:N_HEADS + col + 1] * o_sel[rows]
                     + gates[:, 2 * N_HEADS + col:2 * N_HEADS + col + 1] * o_win[rows])
                o_ref[0, :, col * HEAD_DIM:(col + 1) * HEAD_DIM] = o


def nsa_sample_attend(q, cmp, pages, page_ids, new_rows, band, gates, past_len):
    B, T, _ = q.shape
    npg = PAGES_PER_STEP
    ppb = page_ids.shape[0] // B
    steps = ppb // npg
    ns = (past_len + T + SEL_BLOCK - 1) // SEL_BLOCK
    ns_pad = -(-ns // LANES) * LANES
    o_cmp, sel = pl.pallas_call(
        functools.partial(_nsa_sample_select_kernel, past_len=past_len, ns_pad=ns_pad),
        grid=(B,),
        in_specs=[pl.BlockSpec((1, T, q.shape[2]), lambda b: (b, 0, 0)),
                  pl.BlockSpec((1,) + cmp.shape[1:], lambda b: (b, 0, 0))],
        out_specs=[pl.BlockSpec((1, KV_HEADS, HPG * T, HEAD_DIM), lambda b: (b, 0, 0, 0)),
                   pl.BlockSpec((1, KV_HEADS * T, ns_pad), lambda b: (b, 0, 0))],
        out_shape=[jax.ShapeDtypeStruct((B, KV_HEADS, HPG * T, HEAD_DIM), F32),
                   jax.ShapeDtypeStruct((B, KV_HEADS * T, ns_pad), F32)],
        compiler_params=_cparams(("parallel",)),
        name="nsa_sample_select",
    )(q, cmp)

    def page_spec(p):
        return pl.BlockSpec((1, PAGE_SIZE, 2 * KV_HEADS, HEAD_DIM),
                            lambda b, c, pt: (pt[b * ppb + c * npg + p], 0, 1, 0))

    per_b = lambda a: pl.BlockSpec((1,) + a.shape[1:], lambda b, c, pt: (b,) + (0,) * (a.ndim - 1))
    R = HPG * T
    return pl.pallas_call(
        functools.partial(_nsa_sample_attend_kernel, past_len=past_len, ns_pad=ns_pad),
        grid_spec=pltpu.PrefetchScalarGridSpec(
            num_scalar_prefetch=1,
            grid=(B, steps),
            in_specs=[page_spec(p) for p in range(npg)]
            + [per_b(q), per_b(sel), per_b(new_rows), per_b(band), per_b(o_cmp), per_b(gates)],
            out_specs=pl.BlockSpec((1, T, q.shape[2]), lambda b, c, pt: (b, 0, 0)),
            scratch_shapes=[pltpu.VMEM((KV_HEADS, R, LANES), F32),
                            pltpu.VMEM((KV_HEADS, R, LANES), F32),
                            pltpu.VMEM((KV_HEADS, R, HEAD_DIM), F32)]),
        out_shape=jax.ShapeDtypeStruct(q.shape, F32),
        compiler_params=_cparams(("parallel", "arbitrary")),
        name="nsa_sample_attend",
    )(page_ids, *([pages] * npg), q, sel, new_rows, band, o_cmp, gates)


def _conv_gate_kernel(b_ref, c_ref, h_ref, prev_ref, w_ref, o_ref, st_ref, carry_ref):
    i = pl.program_id(1)
    tt = b_ref.shape[0]

    @pl.when(i == 0)
    def _():
        carry_ref[...] = prev_ref[0]

    u = c_ref[...] * h_ref[...]
    row = lax.broadcasted_iota(jnp.int32, u.shape, 0)
    prev = carry_ref[...]
    u1 = jnp.where(row == 0, prev[7:8, :], pltpu.roll(u, 1, axis=0))
    u2 = jnp.where(row == 0, prev[6:7, :], jnp.where(row == 1, prev[7:8, :], pltpu.roll(u, 2, axis=0)))
    w = w_ref[...]
    conv = w[0:1, :] * u2 + w[1:2, :] * u1 + w[2:3, :] * u
    o_ref[...] = (b_ref[...] * conv).astype(o_ref.dtype)
    carry_ref[...] = u[tt - 8:, :]
    st_ref[0] = u[tt - 8:, :]


def conv_gate(proj, prev8, w_conv, B, T, *, tt, out_dtype):
    D = proj.shape[1] // 3
    nt = T // tt
    col = lambda k: pl.BlockSpec((tt, D), lambda b, i: (b * nt + i, k))
    return pl.pallas_call(
        _conv_gate_kernel,
        grid=(B, nt),
        in_specs=[col(0), col(1), col(2),
                  pl.BlockSpec((1, 8, D), lambda b, i: (b, 0, 0)),
                  pl.BlockSpec((8, D), lambda b, i: (0, 0))],
        out_specs=[pl.BlockSpec((tt, D), lambda b, i: (b * nt + i, 0)),
                   pl.BlockSpec((1, 8, D), lambda b, i: (b, 0, 0))],
        out_shape=[jax.ShapeDtypeStruct((B * T, D), out_dtype), jax.ShapeDtypeStruct((B, 8, D), F32)],
        scratch_shapes=[pltpu.VMEM((8, D), F32)],
        compiler_params=_cparams(("parallel", "arbitrary")),
        name="conv_gate",
    )(proj, proj, proj, prev8, w_conv)


def _cmlp_mix_kernel(u_ref, v_ref, g_ref, b_ref, ws_ref, bs_ref, o_ref, vo_ref):
    c = u_ref.shape[0]
    v = _ln(v_ref[...], g_ref[...], b_ref[...])
    vo_ref[...] = v
    vb = v.astype(BF16)
    dg = v.shape[1] // SG_GROUPS
    tril = lax.broadcasted_iota(jnp.int32, (c, c), 0) >= lax.broadcasted_iota(jnp.int32, (c, c), 1)
    for g in range(SG_GROUPS):
        ws = jnp.where(tril, ws_ref[g], 0.0).astype(BF16)
        mixed = _dot(ws, vb[:, g * dg:(g + 1) * dg]) + bs_ref[g]
        o_ref[:, g * dg:(g + 1) * dg] = (u_ref[:, g * dg:(g + 1) * dg] * mixed).astype(o_ref.dtype)


def cmlp_mix(proj, ln_g, ln_b, w_s, b_s, *, c, out_dtype):
    M = proj.shape[0]
    D = proj.shape[1] // 2
    const = lambda a: pl.BlockSpec(a.shape, lambda i: (0,) * a.ndim)
    ln_g = ln_g.reshape(1, D)
    ln_b = ln_b.reshape(1, D)
    return pl.pallas_call(
        _cmlp_mix_kernel,
        grid=(M // c,),
        in_specs=[pl.BlockSpec((c, D), lambda i: (i, 0)), pl.BlockSpec((c, D), lambda i: (i, 1)),
                  const(ln_g), const(ln_b), const(w_s), const(b_s)],
        out_specs=[pl.BlockSpec((c, D), lambda i: (i, 0)), pl.BlockSpec((c, D), lambda i: (i, 0))],
        out_shape=[jax.ShapeDtypeStruct((M, D), out_dtype), jax.ShapeDtypeStruct((M, D), F32)],
        compiler_params=_cparams(("parallel",)),
        name="cmlp_mix",
    )(proj, proj, ln_g, ln_b, w_s, b_s)


def _router_kernel(x_ref, w_ref, o_ref):
    logits = jnp.dot(x_ref[...], w_ref[...], preferred_element_type=F32, precision=lax.Precision.HIGHEST)
    lane = lax.broadcasted_iota(jnp.int32, logits.shape, 1)
    lane_f = lane.astype(F32)
    lg = jnp.where(lane < N_EXPERTS, logits, -jnp.inf)
    m1 = jnp.max(lg, axis=-1, keepdims=True)
    i1 = jnp.min(jnp.where(lg == m1, lane_f, float(LANES)), axis=-1, keepdims=True)
    lg2 = jnp.where(lane_f == i1, -jnp.inf, lg)
    m2 = jnp.max(lg2, axis=-1, keepdims=True)
    i2 = jnp.min(jnp.where(lg2 == m2, lane_f, float(LANES)), axis=-1, keepdims=True)
    e = jnp.exp(m2 - m1)
    den = 1.0 + e
    g1 = 1.0 / den
    g2 = e / den
    o_ref[...] = jnp.where(lane == 0, i1, jnp.where(lane == 1, i2,
                                                    jnp.where(lane == 2, g1, jnp.where(lane == 3, g2, 0.0))))


def router_top2(x, w_router, *, tm):
    M, K = x.shape
    w = jnp.pad(w_router, ((0, 0), (0, LANES - N_EXPERTS)))
    return pl.pallas_call(
        _router_kernel,
        grid=(M // tm,),
        in_specs=[pl.BlockSpec((tm, K), lambda i: (i, 0)), pl.BlockSpec((K, LANES), lambda i: (0, 0))],
        out_specs=pl.BlockSpec((tm, LANES), lambda i: (i, 0)),
        out_shape=jax.ShapeDtypeStruct((M, LANES), F32),
        compiler_params=_cparams(("parallel",)),
        name="moe_router",
    )(x, w)


def _gmm_swiglu_kernel(be_ref, nv_ref, new_ref, x_ref, wg_ref, wu_ref, o_ref, wgb_ref, wub_ref):
    r = pl.program_id(1)

    @pl.when(r < nv_ref[0])
    def _():
        @pl.when(new_ref[r] == 1)
        def _():
            wgb_ref[...] = wg_ref[0, 0].astype(BF16)
            wub_ref[...] = wu_ref[0, 0].astype(BF16)

        x = x_ref[...]
        g = _dot(x, wgb_ref[...])
        u = _dot(x, wub_ref[...])
        o_ref[...] = (jax.nn.silu(g) * u).astype(o_ref.dtype)


def gmm_swiglu(xs, w_gu, block_e, n_valid, new_expert, *, rb, tn, layer):
    P, K = xs.shape
    F = w_gu.shape[3] // 2
    nf = F // tn
    row = lambda r, nv: jnp.minimum(r, nv[0] - 1)
    return pl.pallas_call(
        _gmm_swiglu_kernel,
        grid_spec=pltpu.PrefetchScalarGridSpec(
            num_scalar_prefetch=3,
            grid=(nf, P // rb),
            in_specs=[pl.BlockSpec((rb, K), lambda n, r, be, nv, nw: (row(r, nv), 0)),
                      pl.BlockSpec((1, 1, K, tn), lambda n, r, be, nv, nw: (layer, be[row(r, nv)], 0, n)),
                      pl.BlockSpec((1, 1, K, tn), lambda n, r, be, nv, nw: (layer, be[row(r, nv)], 0, n + nf))],
            out_specs=pl.BlockSpec((rb, tn), lambda n, r, be, nv, nw: (row(r, nv), n)),
            scratch_shapes=[pltpu.VMEM((K, tn), BF16), pltpu.VMEM((K, tn), BF16)]),
        out_shape=jax.ShapeDtypeStruct((P, F), BF16),
        compiler_params=_cparams(("arbitrary", "arbitrary")),
        name="moe_gmm_swiglu",
    )(block_e, n_valid, new_expert, xs, w_gu, w_gu)


_IT_FIRST, _IT_LAST, _IT_NEW_W, _IT_SKIP = 1, 2, 4, 8


def _gmm_down_kernel(blk_ref, k_ref, e_ref, slot_ref, oblk_ref, flag_ref, h_ref, w_ref, o_ref, wb_ref, acc_ref):
    w = pl.program_id(0)
    f = flag_ref[w]

    @pl.when((f & _IT_SKIP) == 0)
    def _():
        @pl.when((f & _IT_NEW_W) != 0)
        def _():
            wb_ref[...] = w_ref[0, 0].astype(BF16)

        slot = slot_ref[w]
        part = _dot(h_ref[...], wb_ref[...])

        @pl.when((f & _IT_FIRST) != 0)
        def _():
            acc_ref[slot] = part

        @pl.when((f & _IT_FIRST) == 0)
        def _():
            acc_ref[slot] += part

        @pl.when((f & _IT_LAST) != 0)
        def _():
            o_ref[...] = acc_ref[slot]


def gmm_down(h, w_down, items, *, rb, tk, run, layer):
    P, F = h.shape
    D = w_down.shape[3]
    it_blk, it_k, it_e, it_slot, it_oblk, it_flag = items
    n_items = it_blk.shape[0]
    return pl.pallas_call(
        _gmm_down_kernel,
        grid_spec=pltpu.PrefetchScalarGridSpec(
            num_scalar_prefetch=6,
            grid=(n_items,),
            in_specs=[pl.BlockSpec((rb, tk), lambda w, blk, k, e, s, ob, fl: (blk[w], k[w])),
                      pl.BlockSpec((1, 1, tk, D), lambda w, blk, k, e, s, ob, fl: (layer, e[w], k[w], 0))],
            out_specs=pl.BlockSpec((rb, D), lambda w, blk, k, e, s, ob, fl: (ob[w], 0)),
            scratch_shapes=[pltpu.VMEM((tk, D), BF16), pltpu.VMEM((run, rb, D), F32)]),
        out_shape=jax.ShapeDtypeStruct((P, D), F32),
        compiler_params=_cparams(("arbitrary",)),
        name="moe_gmm_down",
    )(it_blk, it_k, it_e, it_slot, it_oblk, it_flag, h, w_down)


def _moe_combine_ln_kernel(res_ref, y0_ref, y1_ref, r_ref, g_ref, b_ref, o_ref, ob_ref):
    r = r_ref[...]
    f = r[:, 2:3] * y0_ref[...] + r[:, 3:4] * y1_ref[...]
    y = _ln(DEEPNORM_ALPHA * res_ref[...] + f, g_ref[...], b_ref[...])
    o_ref[...] = y
    ob_ref[...] = y.astype(BF16)


def moe_combine_ln(res, y0, y1, routed, gain, bias, *, tm):
    M, N = res.shape
    row = pl.BlockSpec((tm, N), lambda i: (i, 0))
    vec = pl.BlockSpec((1, N), lambda i: (0, 0))
    return pl.pallas_call(
        _moe_combine_ln_kernel,
        grid=(M // tm,),
        in_specs=[row, row, row, pl.BlockSpec((tm, LANES), lambda i: (i, 0)), vec, vec],
        out_specs=[row, row],
        out_shape=[jax.ShapeDtypeStruct((M, N), F32), jax.ShapeDtypeStruct((M, N), BF16)],
        compiler_params=_cparams(("parallel",)),
        name="moe_combine_ln",
    )(res, y0, y1, routed, gain.reshape(1, N), bias.reshape(1, N))


def moe_dispatch(top_e, *, rb, run, nk):
    A = top_e.size
    flat_e = top_e.reshape(A)
    experts = jnp.arange(N_EXPERTS, dtype=jnp.int32)
    onehot = (flat_e[:, None] == experts[None, :]).astype(jnp.int32)
    pos_in_e = jnp.sum((jnp.cumsum(onehot, axis=0) - onehot) * onehot, axis=1)
    counts = jnp.sum(onehot, axis=0)
    nblk_e = (counts + rb - 1) // rb
    bend_e = jnp.cumsum(nblk_e)
    bstart_e = bend_e - nblk_e
    dest = (bstart_e[flat_e] * rb + pos_in_e).astype(jnp.int32)
    n_blocks = -(-(A + N_EXPERTS * (rb - 1)) // rb)
    row_tok = jnp.zeros((n_blocks * rb,), jnp.int32).at[dest].set(jnp.arange(A, dtype=jnp.int32) // TOP_K)
    b = jnp.arange(n_blocks, dtype=jnp.int32)
    block_e = jnp.minimum(jnp.sum((bend_e[None, :] <= b[:, None]).astype(jnp.int32), axis=1), N_EXPERTS - 1)
    n_valid = bend_e[-1]
    new_expert = ((b == 0) | (block_e != jnp.roll(block_e, 1))).astype(jnp.int32)
    idx_in_e = b - bstart_e[block_e]
    slot = idx_in_e % run
    first = b - slot
    n_run = jnp.minimum(run, nblk_e[block_e] - (idx_in_e - slot))
    k = jnp.arange(nk, dtype=jnp.int32)[None, :]
    n_items = nk * n_blocks
    w_idx = nk * first[:, None] + k * n_run[:, None] + slot[:, None]
    w_idx = jnp.where((b < n_valid)[:, None], w_idx, n_items).reshape(-1)
    flag = (jnp.where(k == 0, _IT_FIRST, 0) | jnp.where(k == nk - 1, _IT_LAST, 0)
            | jnp.where(slot[:, None] == 0, _IT_NEW_W, 0))
    full = lambda v: jnp.broadcast_to(v, (n_blocks, nk)).reshape(-1).astype(jnp.int32)
    cols = [full(b[:, None]), full(k), full(block_e[:, None]), full(slot[:, None]),
            full(jnp.where(k == nk - 1, b[:, None], first[:, None])), full(flag)]
    items = [jnp.zeros((n_items,), jnp.int32).at[w_idx].set(c, mode="drop") for c in cols]
    w = jnp.arange(n_items, dtype=jnp.int32)
    last = nk * n_valid - 1
    items = [jnp.where(w <= last, c, c[last]) for c in items]
    items[5] = jnp.where(w <= last, items[5], _IT_SKIP)
    return dest, row_tok, block_e, n_valid.reshape(1), new_expert, items


MOE_RB = 768
MOE_RUN = 3
MOE_TN = 512
MOE_TK = 512


def moe_layer(xp, xpb, xs, xsb, w_router, w_gu, w_down, gain, bias, *, layer):
    Mp, Ms = xp.shape[0], xs.shape[0]
    rp = router_top2(xp, w_router, tm=1024)
    rs = router_top2(xs, w_router, tm=Ms)
    top_e = jnp.concatenate([rp[:, 0:2], rs[:, 0:2]], axis=0).astype(jnp.int32)
    nk = w_down.shape[2] // MOE_TK
    dest, row_tok, block_e, n_valid, new_expert, items = moe_dispatch(top_e, rb=MOE_RB, run=MOE_RUN, nk=nk)
    x_rows = jnp.concatenate([xpb, xsb], axis=0)[row_tok]
    h = gmm_swiglu(x_rows, w_gu, block_e, n_valid, new_expert, rb=MOE_RB, tn=MOE_TN, layer=layer)
    y = gmm_down(h, w_down, items, rb=MOE_RB, tk=MOE_TK, run=MOE_RUN, layer=layer)
    dest = dest.reshape(Mp + Ms, TOP_K)
    dp, ds = dest[:Mp], dest[Mp:]
    xp, xpb = moe_combine_ln(xp, y[dp[:, 0]], y[dp[:, 1]], rp, gain, bias, tm=512)
    xs, xsb = moe_combine_ln(xs, y[ds[:, 0]], y[ds[:, 1]], rs, gain, bias, tm=Ms)
    return xp, xpb, xs, xsb


def _compress_weights(cmp_w1, cmp_pe, cmp_w2):
    s = CMP_STRIDE
    wcat = jnp.concatenate([cmp_w1[0, :s], cmp_w1[0, s:], cmp_w1[1, :s], cmp_w1[1, s:]], axis=-1)
    wcat = wcat.reshape(s // 2, 2 * HEAD_DIM, 4 * HEAD_DIM).astype(BF16)
    return wcat, compress_pe_term(cmp_w1, cmp_pe), cmp_w2.astype(BF16)


def _gates_by_group(gates, M):
    g = gates[:, :3 * N_HEADS].reshape(M, 3, KV_HEADS, HPG).transpose(2, 0, 1, 3).reshape(KV_HEADS, M, 3 * HPG)
    return jnp.pad(g, ((0, 0), (0, 0), (0, LANES - 3 * HPG)))


def _gate_weights(w_in, layer):
    c0 = N_HEADS * HEAD_DIM + 6 * KV_WIDTH
    return jnp.pad(_stacked(w_in)[layer, :, c0:], ((0, 0), (0, LANES - 3 * N_HEADS)))


def _sub_block_view(rows):
    return rows.reshape(-1, PAGE_SIZE // CMP_STRIDE, CMP_STRIDE, 4 * KV_HEADS, HEAD_DIM)


def nsa_prompt_mixer(xb, w_in, cmp_w, cos, sin, B, T, *, tm, layer=0):
    M = B * T
    wcat, pe, w2 = cmp_w
    q, rows, win = nsa_proj(xb, w_in, cos, sin, tm=tm, q_dtype=BF16, layer=layer)
    gates = linear(xb, _gate_weights(w_in, layer), tm=tm, tn=LANES, act="sigmoid")
    page_ids = jnp.arange(M // PAGE_SIZE, dtype=jnp.int32)
    cmp = compress(_sub_block_view(rows), page_ids, B, wcat, pe, w2)
    o = nsa_prompt_attend(q, cmp, rows.reshape(B, T, 4 * KV_WIDTH), win.reshape(B, T, 2 * KV_WIDTH),
                          _gates_by_group(gates, M), B, T)
    wn = min(WINDOW, T)
    return (o, rows.reshape(B, T, 4, KV_HEADS, HEAD_DIM),
            win.reshape(B, T, 2, KV_HEADS, HEAD_DIM)[:, T - wn:])


def nsa_sample_mixer(xb, w_in, cmp_w, cos, sin, cache, page_ids, win_buf, B, T, *, layer=0):
    M = B * T
    past_len = page_ids.shape[0] // B * PAGE_SIZE
    wcat, pe, w2 = cmp_w
    q, rows, win = nsa_proj(xb, w_in, cos, sin, tm=M, q_dtype=F32, layer=layer)
    gates = linear(xb, _gate_weights(w_in, layer), tm=M, tn=LANES, act="sigmoid")
    cmp = compress(_sub_block_view(cache), page_ids, B, wcat, pe, w2)
    pages = cache.reshape(-1, PAGE_SIZE, 4 * KV_HEADS, HEAD_DIM)
    rows3 = rows.reshape(B, T, 4 * KV_WIDTH)
    new_rows = jnp.pad(rows3[:, :, 2 * KV_WIDTH:], ((0, 0), (0, LANES - T), (0, 0)))
    nbuf = win_buf.shape[1]
    band = jnp.concatenate([win_buf.reshape(B, nbuf, 2 * KV_WIDTH), win.reshape(B, T, 2 * KV_WIDTH)], axis=1)
    band_pad = jnp.pad(band, ((0, 0), (0, WINDOW + LANES - band.shape[1]), (0, 0)))
    o = nsa_sample_attend(q.reshape(B, T, -1), cmp, pages, page_ids, new_rows, band_pad,
                          gates.reshape(B, T, LANES), past_len)
    return (o.reshape(M, -1), rows.reshape(B, T, 4, KV_HEADS, HEAD_DIM),
            band[:, band.shape[1] - nbuf:].reshape(B, nbuf, 2, KV_HEADS, HEAD_DIM))


def conv_mixer(xb, prev, w_in, w_conv, B, T, *, tm, tt, layer=0):
    keep = CONV_WIDTH - 1
    proj = linear(xb, w_in, tm=tm, tn=512, layer=layer)
    prev8 = jnp.pad(prev, ((0, 0), (8 - keep, 0), (0, 0)))
    w8 = jnp.pad(w_conv, ((0, 8 - CONV_WIDTH), (0, 0)))
    gated, st = conv_gate(proj, prev8, w8, B, T, tt=tt, out_dtype=BF16 if tt % 16 == 0 else F32)
    return gated, st[:, 8 - keep:]


def chunk_mlp_mixer(xb, w_in, ln_g, ln_b, w_s, b_s, B, T, *, tm, layer=0):
    D = w_in.shape[-1] // 2
    c = min(T, CHUNK)
    proj = linear(xb, w_in, tm=tm, tn=512, act="gelu", layer=layer)
    cp = max(c, 16)
    if cp != c:
        proj = jnp.pad(proj.reshape(B * T // c, c, 2 * D), ((0, 0), (0, cp - c), (0, 0))).reshape(-1, 2 * D)
    ws = jnp.pad(w_s[:, :c, :c], ((0, 0), (0, cp - c), (0, cp - c)))
    bs = jnp.pad(b_s[:, :c, None], ((0, 0), (0, cp - c), (0, 0)))
    mixed, v = cmlp_mix(proj, ln_g, ln_b, ws, bs, c=cp, out_dtype=BF16 if c == cp else F32)
    if cp != c:
        mixed = mixed.reshape(-1, cp, D)[:, :c].reshape(B * T, D)
        v = v.reshape(-1, cp, D)[:, :c].reshape(B * T, D)
    return mixed, v.reshape(B, T, D)[:, T - c:]


def kernel(x_prompt, x_sample, cache_nsa_kv, state_nsa_win, state_conv, page_table, ln_gain, ln_bias, nsa_w_in, nsa_cmp_w1, nsa_cmp_pe, nsa_cmp_w2, nsa_w_out, conv_w_in, conv_w, conv_w_out, cmlp_w_in, cmlp_ln_gain, cmlp_ln_bias, cmlp_w_s, cmlp_b_s, cmlp_w_out, ffn_w_gu, ffn_w_down, moe_router, moe_w_gu, moe_w_down):
    B, T, D = x_prompt.shape
    SB, ST, _ = x_sample.shape
    Mp, Ms = B * T, SB * ST
    n_pool = cache_nsa_kv.shape[1]
    n_pages = page_table.shape[1]
    past_len = n_pages * PAGE_SIZE
    tm_p, tm_s = 1024, Ms

    xp = x_prompt.reshape(Mp, D)
    xs = x_sample.reshape(Ms, D)
    xpb, xsb = xp, xs

    pos_p = jnp.tile(jnp.arange(T, dtype=jnp.int32), B)
    pos_s = jnp.tile(past_len + jnp.arange(ST, dtype=jnp.int32), SB)
    cos_p, sin_p = rope_tables(pos_p)
    cos_s, sin_s = rope_tables(pos_s)
    nsa_w_out_b = nsa_w_out.astype(BF16)
    conv_w_out_b = conv_w_out.astype(BF16)
    cmlp_w_out_b = cmlp_w_out.astype(BF16)

    rows_p, rows_s, win_p, win_s = [], [], [], []
    conv_p, conv_s, v_p, v_s = [], [], [], []
    for i in range(DEPTH):
        m = i // 3
        kind = i % 3
        g0, b0 = ln_gain[i, 0], ln_bias[i, 0]
        if kind == 0:
            w = (nsa_w_in, _compress_weights(nsa_cmp_w1[m], nsa_cmp_pe[m], nsa_cmp_w2[m]))
            o, rows, win = nsa_prompt_mixer(xpb, *w, cos_p, sin_p, B, T, tm=tm_p, layer=m)
            xp, xpb = linear_res_ln(o, nsa_w_out_b, xp, g0, b0, tm=512, tk=D, layer=m)
            rows_p.append(rows)
            win_p.append(win)
            page_ids = (m * n_pool + page_table).reshape(-1).astype(jnp.int32)
            o, rows, win = nsa_sample_mixer(xsb, *w, cos_s, sin_s, cache_nsa_kv, page_ids, state_nsa_win[m],
                                            SB, ST, layer=m)
            xs, xsb = linear_res_ln(o, nsa_w_out_b, xs, g0, b0, tm=tm_s, tk=D, layer=m)
            rows_s.append(rows)
            win_s.append(win)
        elif kind == 1:
            w = (conv_w_in, conv_w[m])
            gated, st = conv_mixer(xpb, jnp.zeros((B, CONV_WIDTH - 1, D), F32), *w, B, T, tm=tm_p, tt=256,
                                   layer=m)
            xp, xpb = linear_res_ln(gated, conv_w_out_b, xp, g0, b0, tm=512, tk=D, layer=m)
            conv_p.append(st)
            gated, st = conv_mixer(xsb, state_conv[m], *w, SB, ST, tm=tm_s, tt=ST, layer=m)
            xs, xsb = linear_res_ln(gated, conv_w_out_b, xs, g0, b0, tm=tm_s, tk=D, layer=m)
            conv_s.append(st)
        else:
            w = (cmlp_w_in, cmlp_ln_gain[m], cmlp_ln_bias[m], cmlp_w_s[m], cmlp_b_s[m])
            mixed, v = chunk_mlp_mixer(xpb, *w, B, T, tm=tm_p, layer=m)
            xp, xpb = linear_res_ln(mixed, cmlp_w_out_b, xp, g0, b0, tm=512, tk=D, layer=m)
            v_p.append(v)
            mixed, v = chunk_mlp_mixer(xsb, *w, SB, ST, tm=tm_s, layer=m)
            xs, xsb = linear_res_ln(mixed, cmlp_w_out_b, xs, g0, b0, tm=tm_s, tk=D, layer=m)
            v_s.append(v)
        g1, b1 = ln_gain[i, 1], ln_bias[i, 1]
        f = i // 2
        if i % 2 == 0:
            h = linear_swiglu(xpb, ffn_w_gu, tm=tm_p, tn=512, layer=f)
            xp, xpb = linear_res_ln(h, ffn_w_down, xp, g1, b1, tm=512, tk=512, layer=f)
            h = linear_swiglu(xsb, ffn_w_gu, tm=tm_s, tn=512, layer=f)
            xs, xsb = linear_res_ln(h, ffn_w_down, xs, g1, b1, tm=tm_s, tk=512, layer=f)
        else:
            xp, xpb, xs, xsb = moe_layer(xp, xpb, xs, xsb, moe_router[f], moe_w_gu, moe_w_down, g1, b1,
                                         layer=f)
    return (xp.reshape(B, T, D), xs.reshape(SB, ST, D),
            jnp.stack(rows_p), jnp.stack(rows_s), jnp.stack(win_p), jnp.stack(win_s),
            jnp.stack(conv_p), jnp.stack(conv_s), jnp.stack(v_p), jnp.stack(v_s))
```

```python
import functools

import jax
import jax.numpy as jnp
from jax import lax
from jax.experimental import pallas as pl
from jax.experimental.pallas import tpu as pltpu

D_MODEL = 2048
DEPTH = 4
PAGE_SIZE = 128
N_HEADS = 16
HEAD_DIM = 128
KV_HEADS = 4
HPG = N_HEADS // KV_HEADS
KV_WIDTH = KV_HEADS * HEAD_DIM
CMP_BLOCK = 32
CMP_STRIDE = 16
SEL_BLOCK = 64
SEL_TOPK = 16
WINDOW = 512
ROPE_THETA = 10000.0
FORCE_BONUS = 1e4
CONV_WIDTH = 3
CHUNK = 128
SG_GROUPS = 8
N_EXPERTS = 8
TOP_K = 2
LN_EPS = 1e-5
DEEPNORM_ALPHA = (2 * DEPTH) ** 0.25

LANES = 128
PAGES_PER_STEP = 8
NEG = -1e30
Q_SCALE = HEAD_DIM ** -0.5 * 1.4426950408889634
VMEM_LIMIT = 56 * 1024 * 1024

F32 = jnp.float32
BF16 = jnp.bfloat16


def _cparams(sem):
    return pltpu.CompilerParams(dimension_semantics=sem, vmem_limit_bytes=VMEM_LIMIT)


def _ln(y, g, b):
    mu = jnp.mean(y, axis=-1, keepdims=True)
    d = y - mu
    var = jnp.mean(d * d, axis=-1, keepdims=True)
    return d * lax.rsqrt(var + LN_EPS) * g + b


def _dot(a, b):
    return jnp.dot(a, b, preferred_element_type=F32)


def _dot_nt(a, b, precision=None):
    return lax.dot_general(a, b, (((1,), (1,)), ((), ())), preferred_element_type=F32,
                           precision=precision)


def _stacked(w):
    return w if w.ndim == 3 else w[None]


def _linear_kernel(x_ref, w_ref, o_ref, *, act):
    y = _dot(x_ref[...].astype(BF16), w_ref[0].astype(BF16))
    if act == "gelu":
        y = jax.nn.gelu(y)
    elif act == "sigmoid":
        y = jax.nn.sigmoid(y)
    o_ref[...] = y.astype(o_ref.dtype)


def linear(x, w, *, tm, tn, layer=0, act=None, out_dtype=F32):
    w = _stacked(w)
    M, K = x.shape
    N = w.shape[2]
    return pl.pallas_call(
        functools.partial(_linear_kernel, act=act),
        grid=(M // tm, N // tn),
        in_specs=[pl.BlockSpec((tm, K), lambda i, j: (i, 0)),
                  pl.BlockSpec((1, K, tn), lambda i, j: (layer, 0, j))],
        out_specs=pl.BlockSpec((tm, tn), lambda i, j: (i, j)),
        out_shape=jax.ShapeDtypeStruct((M, N), out_dtype),
        compiler_params=_cparams(("parallel", "arbitrary")),
        name="linear",
    )(x, w)


def _swiglu_kernel(x_ref, wg_ref, wu_ref, o_ref):
    x = x_ref[...].astype(BF16)
    g = _dot(x, wg_ref[0].astype(BF16))
    u = _dot(x, wu_ref[0].astype(BF16))
    o_ref[...] = (jax.nn.silu(g) * u).astype(o_ref.dtype)


def linear_swiglu(x, w_gu, *, tm, tn, layer=0):
    w_gu = _stacked(w_gu)
    M, K = x.shape
    F = w_gu.shape[2] // 2
    nf = F // tn
    return pl.pallas_call(
        _swiglu_kernel,
        grid=(M // tm, nf),
        in_specs=[pl.BlockSpec((tm, K), lambda i, j: (i, 0)),
                  pl.BlockSpec((1, K, tn), lambda i, j: (layer, 0, j)),
                  pl.BlockSpec((1, K, tn), lambda i, j: (layer, 0, j + nf))],
        out_specs=pl.BlockSpec((tm, tn), lambda i, j: (i, j)),
        out_shape=jax.ShapeDtypeStruct((M, F), BF16),
        compiler_params=_cparams(("parallel", "arbitrary")),
        name="linear_swiglu",
    )(x, w_gu, w_gu)


def _linear_res_ln_kernel(h_ref, w_ref, res_ref, g_ref, b_ref, o_ref, ob_ref, *acc):
    def finish(f):
        y = _ln(DEEPNORM_ALPHA * res_ref[...] + f, g_ref[...], b_ref[...])
        o_ref[...] = y
        ob_ref[...] = y.astype(BF16)

    part = _dot(h_ref[...].astype(BF16), w_ref[0].astype(BF16))
    if not acc:
        finish(part)
        return
    acc_ref, = acc
    k = pl.program_id(1)

    @pl.when(k == 0)
    def _():
        acc_ref[...] = part

    @pl.when(k > 0)
    def _():
        acc_ref[...] += part

    @pl.when(k == pl.num_programs(1) - 1)
    def _():
        finish(acc_ref[...])


def linear_res_ln(h, w, res, gain, bias, *, tm, tk, layer=0):
    w = _stacked(w)
    M, K = h.shape
    N = w.shape[2]
    row_mode = dict(pipeline_mode=pl.Buffered(1)) if K != tk else {}
    row = lambda: pl.BlockSpec((tm, N), lambda i, k: (i, 0), **row_mode)
    return pl.pallas_call(
        _linear_res_ln_kernel,
        grid=(M // tm, K // tk),
        in_specs=[pl.BlockSpec((tm, tk), lambda i, k: (i, k)),
                  pl.BlockSpec((1, tk, N), lambda i, k: (layer, k, 0)),
                  row(),
                  pl.BlockSpec((1, N), lambda i, k: (0, 0)),
                  pl.BlockSpec((1, N), lambda i, k: (0, 0))],
        out_specs=[row(), row()],
        out_shape=[jax.ShapeDtypeStruct((M, N), F32), jax.ShapeDtypeStruct((M, N), BF16)],
        scratch_shapes=[pltpu.VMEM((tm, N), F32)] if K != tk else [],
        compiler_params=_cparams(("parallel", "arbitrary")),
        name="linear_res_ln",
    )(h, w, res, gain.reshape(1, N), bias.reshape(1, N))


def _rope_tile(y, cos, sin):
    outs = []
    for c in range(y.shape[1] // HEAD_DIM):
        yc = y[:, c * HEAD_DIM:(c + 1) * HEAD_DIM]
        outs.append(yc * cos + pltpu.roll(yc, HEAD_DIM // 2, axis=1) * sin)
    return jnp.concatenate(outs, axis=1)


def _nsa_proj_kernel(x_ref, w_ref, cos_ref, sin_ref, q_ref, rows_ref, win_ref):
    j = pl.program_id(1)
    y = _dot(x_ref[...].astype(BF16), w_ref[0].astype(BF16))
    cos = cos_ref[...]
    sin = sin_ref[...]

    @pl.when(j < 4)
    def _():
        q_ref[...] = (_rope_tile(y, cos, sin) * Q_SCALE).astype(q_ref.dtype)

    @pl.when((j == 4) | (j == 6))
    def _():
        rows_ref[...] = _rope_tile(y, cos, sin)

    @pl.when((j == 5) | (j == 7))
    def _():
        rows_ref[...] = y

    @pl.when(j == 8)
    def _():
        win_ref[...] = _rope_tile(y, cos, sin)

    @pl.when(j == 9)
    def _():
        win_ref[...] = y


def nsa_proj(x, w_in, cos, sin, *, tm, q_dtype, layer=0):
    w_in = _stacked(w_in)
    M, K = x.shape
    tn = KV_WIDTH
    qd = N_HEADS * HEAD_DIM
    return pl.pallas_call(
        _nsa_proj_kernel,
        grid=(M // tm, 10),
        in_specs=[pl.BlockSpec((tm, K), lambda i, j: (i, 0)),
                  pl.BlockSpec((1, K, tn), lambda i, j: (layer, 0, j)),
                  pl.BlockSpec((tm, HEAD_DIM), lambda i, j: (i, 0)),
                  pl.BlockSpec((tm, HEAD_DIM), lambda i, j: (i, 0))],
        out_specs=[pl.BlockSpec((tm, tn), lambda i, j: (i, jnp.minimum(j, 3))),
                   pl.BlockSpec((tm, tn), lambda i, j: (i, jnp.clip(j - 4, 0, 3))),
                   pl.BlockSpec((tm, tn), lambda i, j: (i, jnp.clip(j - 8, 0, 1)))],
        out_shape=[jax.ShapeDtypeStruct((M, qd), q_dtype),
                   jax.ShapeDtypeStruct((M, 4 * KV_WIDTH), F32),
                   jax.ShapeDtypeStruct((M, 2 * KV_WIDTH), F32)],
        compiler_params=_cparams(("parallel", "arbitrary")),
        name="nsa_proj",
    )(x, w_in, cos, sin)


def rope_tables(pos):
    half = HEAD_DIM // 2
    inv = ROPE_THETA ** (-jnp.arange(half, dtype=F32) / half)
    ang = pos.astype(F32)[:, None] * inv[None, :]
    cos = jnp.cos(ang)
    sin = jnp.sin(ang)
    return jnp.concatenate([cos, cos], -1), jnp.concatenate([-sin, sin], -1)


def _compress_kernel(pt_ref, *refs):
    npg = PAGES_PER_STEP
    page_refs = refs[:npg]
    wcat_ref, pe_ref, w2_ref, o_ref, carry_ref = refs[npg:]
    j = pl.program_id(1)
    nch = 2 * KV_HEADS
    nsub = PAGE_SIZE // CMP_STRIDE
    R = npg * nsub * nch

    @pl.when(j == 0)
    def _():
        carry_ref[...] = jnp.zeros_like(carry_ref)

    def rows_at(l):
        return jnp.concatenate([page_refs[p][0, :, l].reshape(nsub * nch, HEAD_DIM) for p in range(npg)],
                               axis=0).astype(BF16)

    acc = jnp.zeros((R, 4 * HEAD_DIM), F32)
    for l2 in range(CMP_STRIDE // 2):
        x = jnp.concatenate([rows_at(2 * l2), rows_at(2 * l2 + 1)], axis=1)
        acc = acc + _dot(x, wcat_ref[l2])
    outs = []
    for t in range(2):
        a = acc[:, 2 * t * HEAD_DIM:(2 * t + 1) * HEAD_DIM]
        b = acc[:, (2 * t + 1) * HEAD_DIM:2 * (t + 1) * HEAD_DIM]
        a_shift = jnp.concatenate([carry_ref[t], a[:R - nch]], axis=0)
        carry_ref[t] = a[R - nch:]
        h = a_shift + b + pe_ref[t:t + 1, :]
        outs.append(_dot(jax.nn.gelu(h).astype(BF16), w2_ref[t]))
    is_k = lax.broadcasted_iota(jnp.int32, (R, HEAD_DIM), 0) % nch < KV_HEADS
    o_ref[0] = jnp.where(is_k, outs[0], outs[1])


def _compress_pe_kernel(w1_ref, pe_ref, o_ref):
    for t in range(2):
        acc = jnp.zeros((HEAD_DIM, HEAD_DIM), F32)
        for l in range(CMP_BLOCK):
            acc = acc + w1_ref[t, l] * pe_ref[t, l]
        o_ref[t:t + 1, :] = jnp.sum(acc, axis=0, keepdims=True)


def compress_pe_term(cmp_w1, cmp_pe):
    return pl.pallas_call(
        _compress_pe_kernel,
        out_shape=jax.ShapeDtypeStruct((2, HEAD_DIM), F32),
        compiler_params=_cparams(()),
        name="nsa_compress_pe",
    )(cmp_w1, cmp_pe[..., None])


def compress(pages, page_ids, n_batch, wcat, pe, w2):
    npg = PAGES_PER_STEP
    nch = 2 * KV_HEADS
    nsub = PAGE_SIZE // CMP_STRIDE
    ppb = page_ids.shape[0] // n_batch
    steps = ppb // npg
    R = npg * nsub * nch

    def page_spec(p):
        return pl.BlockSpec((1, nsub, CMP_STRIDE, nch, HEAD_DIM),
                            lambda b, j, pt: (pt[b * ppb + j * npg + p], 0, 0, 0, 0))

    const = lambda shape: pl.BlockSpec(shape, lambda b, j, pt: (0,) * len(shape))
    out = pl.pallas_call(
        _compress_kernel,
        grid_spec=pltpu.PrefetchScalarGridSpec(
            num_scalar_prefetch=1,
            grid=(n_batch, steps),
            in_specs=[page_spec(p) for p in range(npg)]
            + [const(wcat.shape), const(pe.shape), const(w2.shape)],
            out_specs=pl.BlockSpec((1, R, HEAD_DIM), lambda b, j, pt: (b, j, 0)),
            scratch_shapes=[pltpu.VMEM((2, nch, HEAD_DIM), F32)]),
        out_shape=jax.ShapeDtypeStruct((n_batch, steps * R, HEAD_DIM), F32),
        compiler_params=_cparams(("parallel", "arbitrary")),
        name="nsa_compress",
    )(page_ids, *([pages] * npg), wcat, pe, w2)
    return out.reshape(n_batch, steps * npg * nsub, nch * HEAD_DIM)


def _softmax_rows(s, mask):
    sm = jnp.where(mask, s, NEG)
    m = jnp.max(sm, axis=-1, keepdims=True)
    p = jnp.where(mask, jnp.exp2(sm - m), 0.0)
    return p * (1.0 / jnp.maximum(jnp.sum(p, axis=-1, keepdims=True), 1e-30))


def _online_step(s, mask, v, m, l, acc):
    sm = jnp.where(mask, s, NEG)
    m_new = jnp.maximum(m, jnp.max(sm, axis=-1, keepdims=True))
    p = jnp.where(mask, jnp.exp2(sm - m_new), 0.0)
    alpha = jnp.exp2(m - m_new)
    l = alpha * l + jnp.sum(p, axis=-1, keepdims=True)
    acc = alpha * acc + _dot(p.astype(BF16), v)
    return m_new, l, acc


def _mask_bias(mask):
    return jnp.where(mask, 0.0, NEG).astype(F32)


def _online_step_biased(s, bias, v, m, l, acc):
    R, C = s.shape
    T = bias.shape[0]
    sm = (s.reshape(R // T, T, C) + bias[None]).reshape(R, C)
    m_new = jnp.maximum(m, jnp.max(sm, axis=-1, keepdims=True))
    p = jnp.exp2(sm - m_new)
    alpha = jnp.exp2(m - m_new)
    l = alpha * l + jnp.sum(p, axis=-1, keepdims=True)
    acc = alpha * acc + _dot(p.astype(BF16), v)
    return m_new, l, acc


def _pool_matrix(n_rows, n_cols, transposed):
    shape = (n_cols, n_rows) if transposed else (n_rows, n_cols)
    n_ax, j_ax = (1, 0) if transposed else (0, 1)
    d = lax.broadcasted_iota(jnp.int32, shape, n_ax) - 4 * lax.broadcasted_iota(jnp.int32, shape, j_ax)
    return jnp.where((d == 0) | (d == 4), 1.0, jnp.where((d > 0) & (d < 4), 2.0, 0.0)).astype(F32)


TQ = 256
SEL_CHUNK = 512


def _nsa_prompt_kernel(q_ref, kc_ref, vc_ref, ks_ref, vs_ref, kw_ref, vw_ref, gate_ref, o_ref):
    qi = pl.program_id(2)
    t0 = qi * TQ
    R = HPG * TQ
    q = q_ref[...]
    qs = jnp.concatenate([q[:, h * HEAD_DIM:(h + 1) * HEAD_DIM] for h in range(HPG)], axis=0)
    t_row = t0 + lax.broadcasted_iota(jnp.int32, (TQ, 1), 0)
    t_rows = jnp.concatenate([t_row] * HPG, axis=0)

    nc = kc_ref.shape[1]
    s = _dot_nt(qs, kc_ref[0].astype(BF16))
    n_idx = lax.broadcasted_iota(jnp.int32, (1, nc), 1)
    cmask = (n_idx >= 1) & (CMP_STRIDE * n_idx + (CMP_STRIDE - 1) <= t_rows)
    p_cmp = _softmax_rows(s, cmask)
    o_cmp = _dot(p_cmp.astype(BF16), vc_ref[0].astype(BF16))
    pg = p_cmp[0:TQ]
    for h in range(1, HPG):
        pg = pg + p_cmp[h * TQ:(h + 1) * TQ]

    ns = ks_ref.shape[1] // SEL_BLOCK
    p_slc = _dot_nt(_pool_matrix(nc, ns, True), pg, precision=lax.Precision.HIGHEST)
    blk = lax.broadcasted_iota(jnp.int32, (ns, TQ), 0)
    cur = (t0 + lax.broadcasted_iota(jnp.int32, (ns, TQ), 1)) // SEL_BLOCK
    avail = blk <= cur
    forced = (blk == 0) | (blk == cur) | (blk == cur - 1)
    score = jnp.where(avail, p_slc + jnp.where(forced, FORCE_BONUS, 0.0), -jnp.inf)
    rank = jnp.zeros((ns, TQ), jnp.int32)
    for jj in range(ns):
        row = score[jj:jj + 1, :]
        beats = (row > score) | ((row == score) & (blk > jj))
        rank = rank + beats.astype(jnp.int32)
    sel_t = jnp.where((rank < SEL_TOPK) & avail, 1.0, 0.0).astype(F32)
    sel_t = jnp.concatenate([sel_t, jnp.zeros((LANES - ns, TQ), F32)], axis=0)
    sel = sel_t.T.astype(BF16)

    def sel_body(c, carry):
        k0 = pl.multiple_of(c * SEL_CHUNK, SEL_CHUNK)
        key = k0 + lax.broadcasted_iota(jnp.int32, (LANES, SEL_CHUNK), 1)
        expand = (lax.broadcasted_iota(jnp.int32, (LANES, SEL_CHUNK), 0) == key // SEL_BLOCK)
        picked = _dot(sel, expand.astype(BF16)) > 0.5
        kpos = k0 + lax.broadcasted_iota(jnp.int32, (1, SEL_CHUNK), 1)
        bias = _mask_bias(picked & (kpos <= t_row))
        k = ks_ref[0, pl.ds(k0, SEL_CHUNK), :].astype(BF16)
        v = vs_ref[0, pl.ds(k0, SEL_CHUNK), :].astype(BF16)
        return _online_step_biased(_dot_nt(qs, k), bias, v, *carry)

    init = (jnp.full((R, 1), NEG, F32), jnp.zeros((R, 1), F32), jnp.zeros((R, HEAD_DIM), F32))
    n_chunks = (t0 + TQ + SEL_CHUNK - 1) // SEL_CHUNK
    _, l, acc = lax.fori_loop(0, n_chunks, sel_body, init)
    o_sel = acc * (1.0 / l)

    band = WINDOW + TQ
    start = pl.multiple_of(jnp.maximum(t0 - WINDOW, 0), TQ)
    kpos = start + lax.broadcasted_iota(jnp.int32, (1, band), 1)
    bias = _mask_bias((kpos <= t_row) & (kpos > t_row - WINDOW))
    s = _dot_nt(qs, kw_ref[0, pl.ds(start, band), :].astype(BF16))
    _, l, acc = _online_step_biased(s, bias, vw_ref[0, pl.ds(start, band), :].astype(BF16), *init)
    o_win = acc * (1.0 / l)

    gates = gate_ref[0]
    for h in range(HPG):
        rows = slice(h * TQ, (h + 1) * TQ)
        o = (gates[:, h:h + 1] * o_cmp[rows] + gates[:, HPG + h:HPG + h + 1] * o_sel[rows]
             + gates[:, 2 * HPG + h:2 * HPG + h + 1] * o_win[rows])
        o_ref[:, h * HEAD_DIM:(h + 1) * HEAD_DIM] = o.astype(o_ref.dtype)


def nsa_prompt_attend(q, cmp, rows, win, gates_g, B, T):
    nq = T // TQ
    kv = lambda col0: pl.BlockSpec((1, T, HEAD_DIM), lambda b, g, i: (b, 0, col0 + g))
    cmp_spec = lambda col0: pl.BlockSpec((1, cmp.shape[1], HEAD_DIM), lambda b, g, i: (b, 0, col0 + g))
    return pl.pallas_call(
        _nsa_prompt_kernel,
        grid=(B, KV_HEADS, nq),
        in_specs=[pl.BlockSpec((TQ, HPG * HEAD_DIM), lambda b, g, i: (b * nq + i, g)),
                  cmp_spec(0), cmp_spec(KV_HEADS),
                  kv(2 * KV_HEADS), kv(3 * KV_HEADS), kv(0), kv(KV_HEADS),
                  pl.BlockSpec((1, TQ, LANES), lambda b, g, i: (g, b * nq + i, 0))],
        out_specs=pl.BlockSpec((TQ, HPG * HEAD_DIM), lambda b, g, i: (b * nq + i, g)),
        out_shape=jax.ShapeDtypeStruct((B * T, N_HEADS * HEAD_DIM), BF16),
        compiler_params=_cparams(("parallel", "parallel", "arbitrary")),
        name="nsa_prompt_attend",
    )(q, cmp, cmp, rows, rows, win, win, gates_g)


def _nsa_sample_select_kernel(q_ref, cmp_ref, ocmp_ref, sel_ref, *, past_len, ns_pad):
    T = q_ref.shape[1]
    R = HPG * T
    nc = cmp_ref.shape[1]
    q = q_ref[0]
    t_pos = past_len + lax.broadcasted_iota(jnp.int32, (T, 1), 0)
    t_rows = jnp.concatenate([t_pos] * HPG, axis=0)
    n_idx = lax.broadcasted_iota(jnp.int32, (1, nc), 1)
    cmask = (n_idx >= 1) & (CMP_STRIDE * n_idx + (CMP_STRIDE - 1) <= t_rows)
    pool = _pool_matrix(nc, ns_pad, False)
    pgs = []
    for g in range(KV_HEADS):
        qs = jnp.concatenate(
            [q[:, (g * HPG + h) * HEAD_DIM:(g * HPG + h + 1) * HEAD_DIM] for h in range(HPG)],
            axis=0).astype(BF16)
        kc = cmp_ref[0, :, g * HEAD_DIM:(g + 1) * HEAD_DIM].astype(BF16)
        vc = cmp_ref[0, :, (KV_HEADS + g) * HEAD_DIM:(KV_HEADS + g + 1) * HEAD_DIM].astype(BF16)
        p = _softmax_rows(_dot_nt(qs, kc), cmask)
        ocmp_ref[0, g] = _dot(p.astype(BF16), vc)
        pg = p[0:T]
        for h in range(1, HPG):
            pg = pg + p[h * T:(h + 1) * T]
        pgs.append(pg)
    pg = jnp.concatenate(pgs, axis=0)
    p_slc = jnp.dot(pg, pool, preferred_element_type=F32, precision=lax.Precision.HIGHEST)
    blk = lax.broadcasted_iota(jnp.int32, (KV_HEADS * T, ns_pad), 1)
    cur = jnp.concatenate([t_pos] * KV_HEADS, axis=0) // SEL_BLOCK
    avail = blk <= cur
    forced = (blk == 0) | (blk == cur) | (blk == cur - 1)
    score = jnp.where(avail, p_slc + jnp.where(forced, FORCE_BONUS, 0.0), -jnp.inf)
    sel = jnp.zeros(score.shape, F32)
    blk_f = blk.astype(F32)
    for _ in range(SEL_TOPK):
        m = jnp.max(score, axis=-1, keepdims=True)
        first = jnp.min(jnp.where(score == m, blk_f, float(ns_pad)), axis=-1, keepdims=True)
        pick = blk_f == first
        sel = jnp.where(pick & (m > -jnp.inf), 1.0, sel)
        score = jnp.where(pick, -jnp.inf, score)
    sel_ref[0] = sel


def _nsa_sample_attend_kernel(pt_ref, *refs, past_len, ns_pad):
    npg = PAGES_PER_STEP
    page_refs = refs[:npg]
    (q_ref, sel_ref, new_ref, band_ref, ocmp_ref, gate_ref, o_ref, m_ref, l_ref, acc_ref) = refs[npg:]
    c = pl.program_id(1)
    T = q_ref.shape[1]
    R = HPG * T
    chunk = npg * PAGE_SIZE
    q = q_ref[0]
    t_pos = past_len + lax.broadcasted_iota(jnp.int32, (T, 1), 0)
    t_rows = jnp.concatenate([t_pos] * HPG, axis=0)

    @pl.when(c == 0)
    def _():
        m_ref[...] = jnp.full(m_ref.shape, NEG, F32)
        l_ref[...] = jnp.zeros_like(l_ref)
        acc_ref[...] = jnp.zeros_like(acc_ref)

    def qs_of(g):
        return jnp.concatenate(
            [q[:, (g * HPG + h) * HEAD_DIM:(g * HPG + h + 1) * HEAD_DIM] for h in range(HPG)],
            axis=0).astype(BF16)

    def picked_masks(first_key, n_keys):
        key = first_key + lax.broadcasted_iota(jnp.int32, (ns_pad, n_keys), 1)
        expand = lax.broadcasted_iota(jnp.int32, (ns_pad, n_keys), 0) == key // SEL_BLOCK
        hit = _dot(sel_ref[0].astype(BF16), expand.astype(BF16))
        kpos = first_key + lax.broadcasted_iota(jnp.int32, (1, n_keys), 1)
        out = []
        for g in range(KV_HEADS):
            m1 = (hit[g * T:(g + 1) * T] > 0.5) & (kpos <= t_pos)
            out.append(jnp.concatenate([m1] * HPG, axis=0))
        return out

    def update(g, s, mask, v):
        m, l, acc = _online_step(s, mask, v, m_ref[g][:, 0:1], l_ref[g][:, 0:1], acc_ref[g])
        m_ref[g] = jnp.broadcast_to(m, (R, LANES))
        l_ref[g] = jnp.broadcast_to(l, (R, LANES))
        acc_ref[g] = acc

    masks = picked_masks(c * chunk, chunk)
    for g in range(KV_HEADS):
        qs = qs_of(g)
        k = jnp.concatenate([page_refs[p][0, :, g, :] for p in range(npg)], axis=0).astype(BF16)
        v = jnp.concatenate([page_refs[p][0, :, KV_HEADS + g, :] for p in range(npg)], axis=0).astype(BF16)
        update(g, _dot_nt(qs, k), masks[g], v)

    @pl.when(c == pl.num_programs(1) - 1)
    def _():
        n_new = new_ref.shape[1]
        n_band = band_ref.shape[1]
        gates = gate_ref[0]
        new_masks = picked_masks(past_len, n_new)
        for g in range(KV_HEADS):
            qs = qs_of(g)
            k = new_ref[0, :, g * HEAD_DIM:(g + 1) * HEAD_DIM].astype(BF16)
            v = new_ref[0, :, KV_WIDTH + g * HEAD_DIM:KV_WIDTH + (g + 1) * HEAD_DIM].astype(BF16)
            update(g, _dot_nt(qs, k), new_masks[g], v)
            o_sel = acc_ref[g] * (1.0 / jnp.maximum(l_ref[g][:, 0:1], 1e-30))
            kpos = (past_len - WINDOW) + lax.broadcasted_iota(jnp.int32, (1, n_band), 1)
            wmask = (kpos <= t_rows) & (kpos > t_rows - WINDOW)
            kw = band_ref[0, :, g * HEAD_DIM:(g + 1) * HEAD_DIM].astype(BF16)
            vw = band_ref[0, :, KV_WIDTH + g * HEAD_DIM:KV_WIDTH + (g + 1) * HEAD_DIM].astype(BF16)
            p_win = _softmax_rows(_dot_nt(qs, kw), wmask)
            o_win = _dot(p_win.astype(BF16), vw)
            o_cmp = ocmp_ref[0, g]
            for h in range(HPG):
                rows = slice(h * T, (h + 1) * T)
                col = g * HPG + h
                o = (gates[:, col:col + 1] * o_cmp[rows]
                     + gates[:, N_HEADS + col:N_HEADS + col + 1] * o_sel[rows]
                     + gates[:, 2 * N_HEADS + col:2 * N_HEADS + col + 1] * o_win[rows])
                o_ref[0, :, col * HEAD_DIM:(col + 1) * HEAD_DIM] = o


def nsa_sample_attend(q, cmp, pages, page_ids, new_rows, band, gates, past_len):
    B, T, _ = q.shape
    npg = PAGES_PER_STEP
    ppb = page_ids.shape[0] // B
    steps = ppb // npg
    ns = (past_len + T + SEL_BLOCK - 1) // SEL_BLOCK
    ns_pad = -(-ns // LANES) * LANES
    o_cmp, sel = pl.pallas_call(
        functools.partial(_nsa_sample_select_kernel, past_len=past_len, ns_pad=ns_pad),
        grid=(B,),
        in_specs=[pl.BlockSpec((1, T, q.shape[2]), lambda b: (b, 0, 0)),
                  pl.BlockSpec((1,) + cmp.shape[1:], lambda b: (b, 0, 0))],
        out_specs=[pl.BlockSpec((1, KV_HEADS, HPG * T, HEAD_DIM), lambda b: (b, 0, 0, 0)),
                   pl.BlockSpec((1, KV_HEADS * T, ns_pad), lambda b: (b, 0, 0))],
        out_shape=[jax.ShapeDtypeStruct((B, KV_HEADS, HPG * T, HEAD_DIM), F32),
                   jax.ShapeDtypeStruct((B, KV_HEADS * T, ns_pad), F32)],
        compiler_params=_cparams(("parallel",)),
        name="nsa_sample_select",
    )(q, cmp)

    def page_spec(p):
        return pl.BlockSpec((1, PAGE_SIZE, 2 * KV_HEADS, HEAD_DIM),
                            lambda b, c, pt: (pt[b * ppb + c * npg + p], 0, 1, 0))

    per_b = lambda a: pl.BlockSpec((1,) + a.shape[1:], lambda b, c, pt: (b,) + (0,) * (a.ndim - 1))
    R = HPG * T
    return pl.pallas_call(
        functools.partial(_nsa_sample_attend_kernel, past_len=past_len, ns_pad=ns_pad),
        grid_spec=pltpu.PrefetchScalarGridSpec(
            num_scalar_prefetch=1,
            grid=(B, steps),
            in_specs=[page_spec(p) for p in range(npg)]
            + [per_b(q), per_b(sel), per_b(new_rows), per_b(band), per_b(o_cmp), per_b(gates)],
            out_specs=pl.BlockSpec((1, T, q.shape[2]), lambda b, c, pt: (b, 0, 0)),
            scratch_shapes=[pltpu.VMEM((KV_HEADS, R, LANES), F32),
                            pltpu.VMEM((KV_HEADS, R, LANES), F32),
                            pltpu.VMEM((KV_HEADS, R, HEAD_DIM), F32)]),
        out_shape=jax.ShapeDtypeStruct(q.shape, F32),
        compiler_params=_cparams(("parallel", "arbitrary")),
        name="nsa_sample_attend",
    )(page_ids, *([pages] * npg), q, sel, new_rows, band, o_cmp, gates)


def _conv_gate_kernel(b_ref, c_ref, h_ref, prev_ref, w_ref, o_ref, st_ref, carry_ref):
    i = pl.program_id(1)
    tt = b_ref.shape[0]

    @pl.when(i == 0)
    def _():
        carry_ref[...] = prev_ref[0]

    u = c_ref[...] * h_ref[...]
    row = lax.broadcasted_iota(jnp.int32, u.shape, 0)
    prev = carry_ref[...]
    u1 = jnp.where(row == 0, prev[7:8, :], pltpu.roll(u, 1, axis=0))
    u2 = jnp.where(row == 0, prev[6:7, :], jnp.where(row == 1, prev[7:8, :], pltpu.roll(u, 2, axis=0)))
    w = w_ref[...]
    conv = w[0:1, :] * u2 + w[1:2, :] * u1 + w[2:3, :] * u
    o_ref[...] = (b_ref[...] * conv).astype(o_ref.dtype)
    carry_ref[...] = u[tt - 8:, :]
    st_ref[0] = u[tt - 8:, :]


def conv_gate(proj, prev8, w_conv, B, T, *, tt, out_dtype):
    D = proj.shape[1] // 3
    nt = T // tt
    col = lambda k: pl.BlockSpec((tt, D), lambda b, i: (b * nt + i, k))
    return pl.pallas_call(
        _conv_gate_kernel,
        grid=(B, nt),
        in_specs=[col(0), col(1), col(2),
                  pl.BlockSpec((1, 8, D), lambda b, i: (b, 0, 0)),
                  pl.BlockSpec((8, D), lambda b, i: (0, 0))],
        out_specs=[pl.BlockSpec((tt, D), lambda b, i: (b * nt + i, 0)),
                   pl.BlockSpec((1, 8, D), lambda b, i: (b, 0, 0))],
        out_shape=[jax.ShapeDtypeStruct((B * T, D), out_dtype), jax.ShapeDtypeStruct((B, 8, D), F32)],
        scratch_shapes=[pltpu.VMEM((8, D), F32)],
        compiler_params=_cparams(("parallel", "arbitrary")),
        name="conv_gate",
    )(proj, proj, proj, prev8, w_conv)


def _cmlp_mix_kernel(u_ref, v_ref, g_ref, b_ref, ws_ref, bs_ref, o_ref, vo_ref):
    c = u_ref.shape[0]
    v = _ln(v_ref[...], g_ref[...], b_ref[...])
    vo_ref[...] = v
    vb = v.astype(BF16)
    dg = v.shape[1] // SG_GROUPS
    tril = lax.broadcasted_iota(jnp.int32, (c, c), 0) >= lax.broadcasted_iota(jnp.int32, (c, c), 1)
    for g in range(SG_GROUPS):
        ws = jnp.where(tril, ws_ref[g], 0.0).astype(BF16)
        mixed = _dot(ws, vb[:, g * dg:(g + 1) * dg]) + bs_ref[g]
        o_ref[:, g * dg:(g + 1) * dg] = (u_ref[:, g * dg:(g + 1) * dg] * mixed).astype(o_ref.dtype)


def cmlp_mix(proj, ln_g, ln_b, w_s, b_s, *, c, out_dtype):
    M = proj.shape[0]
    D = proj.shape[1] // 2
    const = lambda a: pl.BlockSpec(a.shape, lambda i: (0,) * a.ndim)
    ln_g = ln_g.reshape(1, D)
    ln_b = ln_b.reshape(1, D)
    return pl.pallas_call(
        _cmlp_mix_kernel,
        grid=(M // c,),
        in_specs=[pl.BlockSpec((c, D), lambda i: (i, 0)), pl.BlockSpec((c, D), lambda i: (i, 1)),
                  const(ln_g), const(ln_b), const(w_s), const(b_s)],
        out_specs=[pl.BlockSpec((c, D), lambda i: (i, 0)), pl.BlockSpec((c, D), lambda i: (i, 0))],
        out_shape=[jax.ShapeDtypeStruct((M, D), out_dtype), jax.ShapeDtypeStruct((M, D), F32)],
        compiler_params=_cparams(("parallel",)),
        name="cmlp_mix",
    )(proj, proj, ln_g, ln_b, w_s, b_s)


def _router_kernel(x_ref, w_ref, o_ref):
    logits = jnp.dot(x_ref[...], w_ref[...], preferred_element_type=F32, precision=lax.Precision.HIGHEST)
    lane = lax.broadcasted_iota(jnp.int32, logits.shape, 1)
    lane_f = lane.astype(F32)
    lg = jnp.where(lane < N_EXPERTS, logits, -jnp.inf)
    m1 = jnp.max(lg, axis=-1, keepdims=True)
    i1 = jnp.min(jnp.where(lg == m1, lane_f, float(LANES)), axis=-1, keepdims=True)
    lg2 = jnp.where(lane_f == i1, -jnp.inf, lg)
    m2 = jnp.max(lg2, axis=-1, keepdims=True)
    i2 = jnp.min(jnp.where(lg2 == m2, lane_f, float(LANES)), axis=-1, keepdims=True)
    e = jnp.exp(m2 - m1)
    den = 1.0 + e
    g1 = 1.0 / den
    g2 = e / den
    o_ref[...] = jnp.where(lane == 0, i1, jnp.where(lane == 1, i2,
                                                    jnp.where(lane == 2, g1, jnp.where(lane == 3, g2, 0.0))))


def router_top2(x, w_router, *, tm):
    M, K = x.shape
    w = jnp.pad(w_router, ((0, 0), (0, LANES - N_EXPERTS)))
    return pl.pallas_call(
        _router_kernel,
        grid=(M // tm,),
        in_specs=[pl.BlockSpec((tm, K), lambda i: (i, 0)), pl.BlockSpec((K, LANES), lambda i: (0, 0))],
        out_specs=pl.BlockSpec((tm, LANES), lambda i: (i, 0)),
        out_shape=jax.ShapeDtypeStruct((M, LANES), F32),
        compiler_params=_cparams(("parallel",)),
        name="moe_router",
    )(x, w)


def _gmm_swiglu_kernel(be_ref, nv_ref, new_ref, x_ref, wg_ref, wu_ref, o_ref, wgb_ref, wub_ref):
    r = pl.program_id(1)

    @pl.when(r < nv_ref[0])
    def _():
        @pl.when(new_ref[r] == 1)
        def _():
            wgb_ref[...] = wg_ref[0, 0].astype(BF16)
            wub_ref[...] = wu_ref[0, 0].astype(BF16)

        x = x_ref[...]
        g = _dot(x, wgb_ref[...])
        u = _dot(x, wub_ref[...])
        o_ref[...] = (jax.nn.silu(g) * u).astype(o_ref.dtype)


def gmm_swiglu(xs, w_gu, block_e, n_valid, new_expert, *, rb, tn, layer):
    P, K = xs.shape
    F = w_gu.shape[3] // 2
    nf = F // tn
    row = lambda r, nv: jnp.minimum(r, nv[0] - 1)
    return pl.pallas_call(
        _gmm_swiglu_kernel,
        grid_spec=pltpu.PrefetchScalarGridSpec(
            num_scalar_prefetch=3,
            grid=(nf, P // rb),
            in_specs=[pl.BlockSpec((rb, K), lambda n, r, be, nv, nw: (row(r, nv), 0)),
                      pl.BlockSpec((1, 1, K, tn), lambda n, r, be, nv, nw: (layer, be[row(r, nv)], 0, n)),
                      pl.BlockSpec((1, 1, K, tn), lambda n, r, be, nv, nw: (layer, be[row(r, nv)], 0, n + nf))],
            out_specs=pl.BlockSpec((rb, tn), lambda n, r, be, nv, nw: (row(r, nv), n)),
            scratch_shapes=[pltpu.VMEM((K, tn), BF16), pltpu.VMEM((K, tn), BF16)]),
        out_shape=jax.ShapeDtypeStruct((P, F), BF16),
        compiler_params=_cparams(("arbitrary", "arbitrary")),
        name="moe_gmm_swiglu",
    )(block_e, n_valid, new_expert, xs, w_gu, w_gu)


_IT_FIRST, _IT_LAST, _IT_NEW_W, _IT_SKIP = 1, 2, 4, 8


def _gmm_down_kernel(blk_ref, k_ref, e_ref, slot_ref, oblk_ref, flag_ref, h_ref, w_ref, o_ref, wb_ref, acc_ref):
    w = pl.program_id(0)
    f = flag_ref[w]

    @pl.when((f & _IT_SKIP) == 0)
    def _():
        @pl.when((f & _IT_NEW_W) != 0)
        def _():
            wb_ref[...] = w_ref[0, 0].astype(BF16)

        slot = slot_ref[w]
        part = _dot(h_ref[...], wb_ref[...])

        @pl.when((f & _IT_FIRST) != 0)
        def _():
            acc_ref[slot] = part

        @pl.when((f & _IT_FIRST) == 0)
        def _():
            acc_ref[slot] += part

        @pl.when((f & _IT_LAST) != 0)
        def _():
            o_ref[...] = acc_ref[slot]


def gmm_down(h, w_down, items, *, rb, tk, run, layer):
    P, F = h.shape
    D = w_down.shape[3]
    it_blk, it_k, it_e, it_slot, it_oblk, it_flag = items
    n_items = it_blk.shape[0]
    return pl.pallas_call(
        _gmm_down_kernel,
        grid_spec=pltpu.PrefetchScalarGridSpec(
            num_scalar_prefetch=6,
            grid=(n_items,),
            in_specs=[pl.BlockSpec((rb, tk), lambda w, blk, k, e, s, ob, fl: (blk[w], k[w])),
                      pl.BlockSpec((1, 1, tk, D), lambda w, blk, k, e, s, ob, fl: (layer, e[w], k[w], 0))],
            out_specs=pl.BlockSpec((rb, D), lambda w, blk, k, e, s, ob, fl: (ob[w], 0)),
            scratch_shapes=[pltpu.VMEM((tk, D), BF16), pltpu.VMEM((run, rb, D), F32)]),
        out_shape=jax.ShapeDtypeStruct((P, D), F32),
        compiler_params=_cparams(("arbitrary",)),
        name="moe_gmm_down",
    )(it_blk, it_k, it_e, it_slot, it_oblk, it_flag, h, w_down)


def _moe_combine_ln_kernel(res_ref, y0_ref, y1_ref, r_ref, g_ref, b_ref, o_ref, ob_ref):
    r = r_ref[...]
    f = r[:, 2:3] * y0_ref[...] + r[:, 3:4] * y1_ref[...]
    y = _ln(DEEPNORM_ALPHA * res_ref[...] + f, g_ref[...], b_ref[...])
    o_ref[...] = y
    ob_ref[...] = y.astype(BF16)


def moe_combine_ln(res, y0, y1, routed, gain, bias, *, tm):
    M, N = res.shape
    row = pl.BlockSpec((tm, N), lambda i: (i, 0))
    vec = pl.BlockSpec((1, N), lambda i: (0, 0))
    return pl.pallas_call(
        _moe_combine_ln_kernel,
        grid=(M // tm,),
        in_specs=[row, row, row, pl.BlockSpec((tm, LANES), lambda i: (i, 0)), vec, vec],
        out_specs=[row, row],
        out_shape=[jax.ShapeDtypeStruct((M, N), F32), jax.ShapeDtypeStruct((M, N), BF16)],
        compiler_params=_cparams(("parallel",)),
        name="moe_combine_ln",
    )(res, y0, y1, routed, gain.reshape(1, N), bias.reshape(1, N))


def moe_dispatch(top_e, *, rb, run, nk):
    A = top_e.size
    flat_e = top_e.reshape(A)
    experts = jnp.arange(N_EXPERTS, dtype=jnp.int32)
    onehot = (flat_e[:, None] == experts[None, :]).astype(jnp.int32)
    pos_in_e = jnp.sum((jnp.cumsum(onehot, axis=0) - onehot) * onehot, axis=1)
    counts = jnp.sum(onehot, axis=0)
    nblk_e = (counts + rb - 1) // rb
    bend_e = jnp.cumsum(nblk_e)
    bstart_e = bend_e - nblk_e
    dest = (bstart_e[flat_e] * rb + pos_in_e).astype(jnp.int32)
    n_blocks = -(-(A + N_EXPERTS * (rb - 1)) // rb)
    row_tok = jnp.zeros((n_blocks * rb,), jnp.int32).at[dest].set(jnp.arange(A, dtype=jnp.int32) // TOP_K)
    b = jnp.arange(n_blocks, dtype=jnp.int32)
    block_e = jnp.minimum(jnp.sum((bend_e[None, :] <= b[:, None]).astype(jnp.int32), axis=1), N_EXPERTS - 1)
    n_valid = bend_e[-1]
    new_expert = ((b == 0) | (block_e != jnp.roll(block_e, 1))).astype(jnp.int32)
    idx_in_e = b - bstart_e[block_e]
    slot = idx_in_e % run
    first = b - slot
    n_run = jnp.minimum(run, nblk_e[block_e] - (idx_in_e - slot))
    k = jnp.arange(nk, dtype=jnp.int32)[None, :]
    n_items = nk * n_blocks
    w_idx = nk * first[:, None] + k * n_run[:, None] + slot[:, None]
    w_idx = jnp.where((b < n_valid)[:, None], w_idx, n_items).reshape(-1)
    flag = (jnp.where(k == 0, _IT_FIRST, 0) | jnp.where(k == nk - 1, _IT_LAST, 0)
            | jnp.where(slot[:, None] == 0, _IT_NEW_W, 0))
    full = lambda v: jnp.broadcast_to(v, (n_blocks, nk)).reshape(-1).astype(jnp.int32)
    cols = [full(b[:, None]), full(k), full(block_e[:, None]), full(slot[:, None]),
            full(jnp.where(k == nk - 1, b[:, None], first[:, None])), full(flag)]
    items = [jnp.zeros((n_items,), jnp.int32).at[w_idx].set(c, mode="drop") for c in cols]
    w = jnp.arange(n_items, dtype=jnp.int32)
    last = nk * n_valid - 1
    items = [jnp.where(w <= last, c, c[last]) for c in items]
    items[5] = jnp.where(w <= last, items[5], _IT_SKIP)
    return dest, row_tok, block_e, n_valid.reshape(1), new_expert, items


MOE_RB = 384
MOE_RUN = 6
MOE_TN = 1024
MOE_TK = 1024


def moe_layer(xp, xpb, xs, xsb, w_router, w_gu, w_down, gain, bias, *, layer):
    Mp, Ms = xp.shape[0], xs.shape[0]
    rp = router_top2(xp, w_router, tm=1024)
    rs = router_top2(xs, w_router, tm=Ms)
    top_e = jnp.concatenate([rp[:, 0:2], rs[:, 0:2]], axis=0).astype(jnp.int32)
    nk = w_down.shape[2] // MOE_TK
    dest, row_tok, block_e, n_valid, new_expert, items = moe_dispatch(top_e, rb=MOE_RB, run=MOE_RUN, nk=nk)
    x_rows = jnp.concatenate([xpb, xsb], axis=0)[row_tok]
    h = gmm_swiglu(x_rows, w_gu, block_e, n_valid, new_expert, rb=MOE_RB, tn=MOE_TN, layer=layer)
    y = gmm_down(h, w_down, items, rb=MOE_RB, tk=MOE_TK, run=MOE_RUN, layer=layer)
    dest = dest.reshape(Mp + Ms, TOP_K)
    dp, ds = dest[:Mp], dest[Mp:]
    xp, xpb = moe_combine_ln(xp, y[dp[:, 0]], y[dp[:, 1]], rp, gain, bias, tm=512)
    xs, xsb = moe_combine_ln(xs, y[ds[:, 0]], y[ds[:, 1]], rs, gain, bias, tm=Ms)
    return xp, xpb, xs, xsb


def _compress_weights(cmp_w1, cmp_pe, cmp_w2):
    s = CMP_STRIDE
    wcat = jnp.concatenate([cmp_w1[0, :s], cmp_w1[0, s:], cmp_w1[1, :s], cmp_w1[1, s:]], axis=-1)
    wcat = wcat.reshape(s // 2, 2 * HEAD_DIM, 4 * HEAD_DIM).astype(BF16)
    return wcat, compress_pe_term(cmp_w1, cmp_pe), cmp_w2.astype(BF16)


def _gates_by_group(gates, M):
    g = gates[:, :3 * N_HEADS].reshape(M, 3, KV_HEADS, HPG).transpose(2, 0, 1, 3).reshape(KV_HEADS, M, 3 * HPG)
    return jnp.pad(g, ((0, 0), (0, 0), (0, LANES - 3 * HPG)))


def _gate_weights(w_in, layer):
    c0 = N_HEADS * HEAD_DIM + 6 * KV_WIDTH
    return jnp.pad(_stacked(w_in)[layer, :, c0:], ((0, 0), (0, LANES - 3 * N_HEADS)))


def _sub_block_view(rows):
    return rows.reshape(-1, PAGE_SIZE // CMP_STRIDE, CMP_STRIDE, 4 * KV_HEADS, HEAD_DIM)


def nsa_prompt_mixer(xb, w_in, cmp_w, cos, sin, B, T, *, tm, layer=0):
    M = B * T
    wcat, pe, w2 = cmp_w
    q, rows, win = nsa_proj(xb, w_in, cos, sin, tm=tm, q_dtype=BF16, layer=layer)
    gates = linear(xb, _gate_weights(w_in, layer), tm=tm, tn=LANES, act="sigmoid")
    page_ids = jnp.arange(M // PAGE_SIZE, dtype=jnp.int32)
    cmp = compress(_sub_block_view(rows), page_ids, B, wcat, pe, w2)
    o = nsa_prompt_attend(q, cmp, rows.reshape(B, T, 4 * KV_WIDTH), win.reshape(B, T, 2 * KV_WIDTH),
                          _gates_by_group(gates, M), B, T)
    wn = min(WINDOW, T)
    return (o, rows.reshape(B, T, 4, KV_HEADS, HEAD_DIM),
            win.reshape(B, T, 2, KV_HEADS, HEAD_DIM)[:, T - wn:])


def nsa_sample_mixer(xb, w_in, cmp_w, cos, sin, cache, page_ids, win_buf, B, T, *, layer=0):
    M = B * T
    past_len = page_ids.shape[0] // B * PAGE_SIZE
    wcat, pe, w2 = cmp_w
    q, rows, win = nsa_proj(xb, w_in, cos, sin, tm=M, q_dtype=F32, layer=layer)
    gates = linear(xb, _gate_weights(w_in, layer), tm=M, tn=LANES, act="sigmoid")
    cmp = compress(_sub_block_view(cache), page_ids, B, wcat, pe, w2)
    pages = cache.reshape(-1, PAGE_SIZE, 4 * KV_HEADS, HEAD_DIM)
    rows3 = rows.reshape(B, T, 4 * KV_WIDTH)
    new_rows = jnp.pad(rows3[:, :, 2 * KV_WIDTH:], ((0, 0), (0, LANES - T), (0, 0)))
    nbuf = win_buf.shape[1]
    band = jnp.concatenate([win_buf.reshape(B, nbuf, 2 * KV_WIDTH), win.reshape(B, T, 2 * KV_WIDTH)], axis=1)
    band_pad = jnp.pad(band, ((0, 0), (0, WINDOW + LANES - band.shape[1]), (0, 0)))
    o = nsa_sample_attend(q.reshape(B, T, -1), cmp, pages, page_ids, new_rows, band_pad,
                          gates.reshape(B, T, LANES), past_len)
    return (o.reshape(M, -1), rows.reshape(B, T, 4, KV_HEADS, HEAD_DIM),
            band[:, band.shape[1] - nbuf:].reshape(B, nbuf, 2, KV_HEADS, HEAD_DIM))


def conv_mixer(xb, prev, w_in, w_conv, B, T, *, tm, tt, layer=0):
    keep = CONV_WIDTH - 1
    proj = linear(xb, w_in, tm=tm, tn=512, layer=layer)
    prev8 = jnp.pad(prev, ((0, 0), (8 - keep, 0), (0, 0)))
    w8 = jnp.pad(w_conv, ((0, 8 - CONV_WIDTH), (0, 0)))
    gated, st = conv_gate(proj, prev8, w8, B, T, tt=tt, out_dtype=BF16 if tt % 16 == 0 else F32)
    return gated, st[:, 8 - keep:]


def chunk_mlp_mixer(xb, w_in, ln_g, ln_b, w_s, b_s, B, T, *, tm, layer=0):
    D = w_in.shape[-1] // 2
    c = min(T, CHUNK)
    proj = linear(xb, w_in, tm=tm, tn=512, act="gelu", layer=layer)
    cp = max(c, 16)
    if cp != c:
        proj = jnp.pad(proj.reshape(B * T // c, c, 2 * D), ((0, 0), (0, cp - c), (0, 0))).reshape(-1, 2 * D)
    ws = jnp.pad(w_s[:, :c, :c], ((0, 0), (0, cp - c), (0, cp - c)))
    bs = jnp.pad(b_s[:, :c, None], ((0, 0), (0, cp - c), (0, 0)))
    mixed, v = cmlp_mix(proj, ln_g, ln_b, ws, bs, c=cp, out_dtype=BF16 if c == cp else F32)
    if cp != c:
        mixed = mixed.reshape(-1, cp, D)[:, :c].reshape(B * T, D)
        v = v.reshape(-1, cp, D)[:, :c].reshape(B * T, D)
    return mixed, v.reshape(B, T, D)[:, T - c:]


def kernel(x_prompt, x_sample, cache_nsa_kv, state_nsa_win, state_conv, page_table, ln_gain, ln_bias, nsa_w_in, nsa_cmp_w1, nsa_cmp_pe, nsa_cmp_w2, nsa_w_out, conv_w_in, conv_w, conv_w_out, cmlp_w_in, cmlp_ln_gain, cmlp_ln_bias, cmlp_w_s, cmlp_b_s, cmlp_w_out, ffn_w_gu, ffn_w_down, moe_router, moe_w_gu, moe_w_down):
    B, T, D = x_prompt.shape
    SB, ST, _ = x_sample.shape
    Mp, Ms = B * T, SB * ST
    n_pool = cache_nsa_kv.shape[1]
    n_pages = page_table.shape[1]
    past_len = n_pages * PAGE_SIZE
    tm_p, tm_s = 1024, Ms

    xp = x_prompt.reshape(Mp, D)
    xs = x_sample.reshape(Ms, D)
    xpb, xsb = xp, xs

    pos_p = jnp.tile(jnp.arange(T, dtype=jnp.int32), B)
    pos_s = jnp.tile(past_len + jnp.arange(ST, dtype=jnp.int32), SB)
    cos_p, sin_p = rope_tables(pos_p)
    cos_s, sin_s = rope_tables(pos_s)
    nsa_w_out_b = nsa_w_out.astype(BF16)
    conv_w_out_b = conv_w_out.astype(BF16)
    cmlp_w_out_b = cmlp_w_out.astype(BF16)

    rows_p, rows_s, win_p, win_s = [], [], [], []
    conv_p, conv_s, v_p, v_s = [], [], [], []
    for i in range(DEPTH):
        m = i // 3
        kind = i % 3
        g0, b0 = ln_gain[i, 0], ln_bias[i, 0]
        if kind == 0:
            w = (nsa_w_in, _compress_weights(nsa_cmp_w1[m], nsa_cmp_pe[m], nsa_cmp_w2[m]))
            o, rows, win = nsa_prompt_mixer(xpb, *w, cos_p, sin_p, B, T, tm=tm_p, layer=m)
            xp, xpb = linear_res_ln(o, nsa_w_out_b, xp, g0, b0, tm=512, tk=D, layer=m)
            rows_p.append(rows)
            win_p.append(win)
            page_ids = (m * n_pool + page_table).reshape(-1).astype(jnp.int32)
            o, rows, win = nsa_sample_mixer(xsb, *w, cos_s, sin_s, cache_nsa_kv, page_ids, state_nsa_win[m],
                                            SB, ST, layer=m)
            xs, xsb = linear_res_ln(o, nsa_w_out_b, xs, g0, b0, tm=tm_s, tk=D, layer=m)
            rows_s.append(rows)
            win_s.append(win)
        elif kind == 1:
            w = (conv_w_in, conv_w[m])
            gated, st = conv_mixer(xpb, jnp.zeros((B, CONV_WIDTH - 1, D), F32), *w, B, T, tm=tm_p, tt=256,
                                   layer=m)
            xp, xpb = linear_res_ln(gated, conv_w_out_b, xp, g0, b0, tm=512, tk=D, layer=m)
            conv_p.append(st)
            gated, st = conv_mixer(xsb, state_conv[m], *w, SB, ST, tm=tm_s, tt=ST, layer=m)
            xs, xsb = linear_res_ln(gated, conv_w_out_b, xs, g0, b0, tm=tm_s, tk=D, layer=m)
            conv_s.append(st)
        else:
            w = (cmlp_w_in, cmlp_ln_gain[m], cmlp_ln_bias[m], cmlp_w_s[m], cmlp_b_s[m])
            mixed, v = chunk_mlp_mixer(xpb, *w, B, T, tm=tm_p, layer=m)
            xp, xpb = linear_res_ln(mixed, cmlp_w_out_b, xp, g0, b0, tm=512, tk=D, layer=m)
            v_p.append(v)
            mixed, v = chunk_mlp_mixer(xsb, *w, SB, ST, tm=tm_s, layer=m)
            xs, xsb = linear_res_ln(mixed, cmlp_w_out_b, xs, g0, b0, tm=tm_s, tk=D, layer=m)
            v_s.append(v)
        g1, b1 = ln_gain[i, 1], ln_bias[i, 1]
        f = i // 2
        if i % 2 == 0:
            h = linear_swiglu(xpb, ffn_w_gu, tm=tm_p, tn=512, layer=f)
            xp, xpb = linear_res_ln(h, ffn_w_down, xp, g1, b1, tm=1024, tk=512, layer=f)
            h = linear_swiglu(xsb, ffn_w_gu, tm=tm_s, tn=512, layer=f)
            xs, xsb = linear_res_ln(h, ffn_w_down, xs, g1, b1, tm=tm_s, tk=512, layer=f)
        else:
            xp, xpb, xs, xsb = moe_layer(xp, xpb, xs, xsb, moe_router[f], moe_w_gu, moe_w_down, g1, b1,
                                         layer=f)
    return (xp.reshape(B, T, D), xs.reshape(SB, ST, D),
            jnp.stack(rows_p), jnp.stack(rows_s), jnp.stack(win_p), jnp.stack(win_s),
            jnp.stack(conv_p), jnp.stack(conv_s), jnp.stack(v_p), jnp.stack(v_s))
```

```python
import functools

import jax
import jax.numpy as jnp
from jax import lax
from jax.experimental import pallas as pl
from jax.experimental.pallas import tpu as pltpu

D_MODEL = 2048
DEPTH = 4
PAGE_SIZE = 128
N_HEADS = 16
HEAD_DIM = 128
KV_HEADS = 4
HPG = N_HEADS // KV_HEADS
KV_WIDTH = KV_HEADS * HEAD_DIM
CMP_BLOCK = 32
CMP_STRIDE = 16
SEL_BLOCK = 64
SEL_TOPK = 16
WINDOW = 512
ROPE_THETA = 10000.0
FORCE_BONUS = 1e4
CONV_WIDTH = 3
CHUNK = 128
SG_GROUPS = 8
N_EXPERTS = 8
TOP_K = 2
LN_EPS = 1e-5
DEEPNORM_ALPHA = (2 * DEPTH) ** 0.25

LANES = 128
PAGES_PER_STEP = 8
NEG = -1e30
Q_SCALE = HEAD_DIM ** -0.5 * 1.4426950408889634
VMEM_LIMIT = 56 * 1024 * 1024

F32 = jnp.float32
BF16 = jnp.bfloat16


def _cparams(sem):
    return pltpu.CompilerParams(dimension_semantics=sem, vmem_limit_bytes=VMEM_LIMIT)


def _ln(y, g, b):
    mu = jnp.mean(y, axis=-1, keepdims=True)
    d = y - mu
    var = jnp.mean(d * d, axis=-1, keepdims=True)
    return d * lax.rsqrt(var + LN_EPS) * g + b


def _dot(a, b):
    return jnp.dot(a, b, preferred_element_type=F32)


def _dot_nt(a, b, precision=None):
    return lax.dot_general(a, b, (((1,), (1,)), ((), ())), preferred_element_type=F32,
                           precision=precision)


def _stacked(w):
    return w if w.ndim == 3 else w[None]


def _linear_kernel(x_ref, w_ref, o_ref, *, act):
    y = _dot(x_ref[...].astype(BF16), w_ref[0].astype(BF16))
    if act == "gelu":
        y = jax.nn.gelu(y)
    elif act == "sigmoid":
        y = jax.nn.sigmoid(y)
    o_ref[...] = y.astype(o_ref.dtype)


def linear(x, w, *, tm, tn, layer=0, act=None, out_dtype=F32):
    w = _stacked(w)
    M, K = x.shape
    N = w.shape[2]
    return pl.pallas_call(
        functools.partial(_linear_kernel, act=act),
        grid=(M // tm, N // tn),
        in_specs=[pl.BlockSpec((tm, K), lambda i, j: (i, 0)),
                  pl.BlockSpec((1, K, tn), lambda i, j: (layer, 0, j))],
        out_specs=pl.BlockSpec((tm, tn), lambda i, j: (i, j)),
        out_shape=jax.ShapeDtypeStruct((M, N), out_dtype),
        compiler_params=_cparams(("parallel", "arbitrary")),
        name="linear",
    )(x, w)


def _swiglu_kernel(x_ref, wg_ref, wu_ref, o_ref):
    x = x_ref[...].astype(BF16)
    g = _dot(x, wg_ref[0].astype(BF16))
    u = _dot(x, wu_ref[0].astype(BF16))
    o_ref[...] = (jax.nn.silu(g) * u).astype(o_ref.dtype)


def linear_swiglu(x, w_gu, *, tm, tn, layer=0):
    w_gu = _stacked(w_gu)
    M, K = x.shape
    F = w_gu.shape[2] // 2
    nf = F // tn
    return pl.pallas_call(
        _swiglu_kernel,
        grid=(M // tm, nf),
        in_specs=[pl.BlockSpec((tm, K), lambda i, j: (i, 0)),
                  pl.BlockSpec((1, K, tn), lambda i, j: (layer, 0, j)),
                  pl.BlockSpec((1, K, tn), lambda i, j: (layer, 0, j + nf))],
        out_specs=pl.BlockSpec((tm, tn), lambda i, j: (i, j)),
        out_shape=jax.ShapeDtypeStruct((M, F), BF16),
        compiler_params=_cparams(("parallel", "arbitrary")),
        name="linear_swiglu",
    )(x, w_gu, w_gu)


def _linear_res_ln_kernel(h_ref, w_ref, res_ref, g_ref, b_ref, o_ref, ob_ref, *acc):
    def finish(f):
        y = _ln(DEEPNORM_ALPHA * res_ref[...] + f, g_ref[...], b_ref[...])
        o_ref[...] = y
        ob_ref[...] = y.astype(BF16)

    part = _dot(h_ref[...].astype(BF16), w_ref[0].astype(BF16))
    if not acc:
        finish(part)
        return
    acc_ref, = acc
    k = pl.program_id(1)

    @pl.when(k == 0)
    def _():
        acc_ref[...] = part

    @pl.when(k > 0)
    def _():
        acc_ref[...] += part

    @pl.when(k == pl.num_programs(1) - 1)
    def _():
        finish(acc_ref[...])


def linear_res_ln(h, w, res, gain, bias, *, tm, tk, layer=0):
    w = _stacked(w)
    M, K = h.shape
    N = w.shape[2]
    row_mode = dict(pipeline_mode=pl.Buffered(1)) if K != tk else {}
    row = lambda: pl.BlockSpec((tm, N), lambda i, k: (i, 0), **row_mode)
    return pl.pallas_call(
        _linear_res_ln_kernel,
        grid=(M // tm, K // tk),
        in_specs=[pl.BlockSpec((tm, tk), lambda i, k: (i, k)),
                  pl.BlockSpec((1, tk, N), lambda i, k: (layer, k, 0)),
                  row(),
                  pl.BlockSpec((1, N), lambda i, k: (0, 0)),
                  pl.BlockSpec((1, N), lambda i, k: (0, 0))],
        out_specs=[row(), row()],
        out_shape=[jax.ShapeDtypeStruct((M, N), F32), jax.ShapeDtypeStruct((M, N), BF16)],
        scratch_shapes=[pltpu.VMEM((tm, N), F32)] if K != tk else [],
        compiler_params=_cparams(("parallel", "arbitrary")),
        name="linear_res_ln",
    )(h, w, res, gain.reshape(1, N), bias.reshape(1, N))


def _rope_tile(y, cos, sin):
    outs = []
    for c in range(y.shape[1] // HEAD_DIM):
        yc = y[:, c * HEAD_DIM:(c + 1) * HEAD_DIM]
        outs.append(yc * cos + pltpu.roll(yc, HEAD_DIM // 2, axis=1) * sin)
    return jnp.concatenate(outs, axis=1)


def _nsa_proj_kernel(x_ref, w_ref, cos_ref, sin_ref, q_ref, rows_ref, win_ref):
    j = pl.program_id(1)
    y = _dot(x_ref[...].astype(BF16), w_ref[0].astype(BF16))
    cos = cos_ref[...]
    sin = sin_ref[...]

    @pl.when(j < 4)
    def _():
        q_ref[...] = (_rope_tile(y, cos, sin) * Q_SCALE).astype(q_ref.dtype)

    @pl.when((j == 4) | (j == 6))
    def _():
        rows_ref[...] = _rope_tile(y, cos, sin)

    @pl.when((j == 5) | (j == 7))
    def _():
        rows_ref[...] = y

    @pl.when(j == 8)
    def _():
        win_ref[...] = _rope_tile(y, cos, sin)

    @pl.when(j == 9)
    def _():
        win_ref[...] = y


def nsa_proj(x, w_in, cos, sin, *, tm, q_dtype, layer=0):
    w_in = _stacked(w_in)
    M, K = x.shape
    tn = KV_WIDTH
    qd = N_HEADS * HEAD_DIM
    return pl.pallas_call(
        _nsa_proj_kernel,
        grid=(M // tm, 10),
        in_specs=[pl.BlockSpec((tm, K), lambda i, j: (i, 0)),
                  pl.BlockSpec((1, K, tn), lambda i, j: (layer, 0, j)),
                  pl.BlockSpec((tm, HEAD_DIM), lambda i, j: (i, 0)),
                  pl.BlockSpec((tm, HEAD_DIM), lambda i, j: (i, 0))],
        out_specs=[pl.BlockSpec((tm, tn), lambda i, j: (i, jnp.minimum(j, 3))),
                   pl.BlockSpec((tm, tn), lambda i, j: (i, jnp.clip(j - 4, 0, 3))),
                   pl.BlockSpec((tm, tn), lambda i, j: (i, jnp.clip(j - 8, 0, 1)))],
        out_shape=[jax.ShapeDtypeStruct((M, qd), q_dtype),
                   jax.ShapeDtypeStruct((M, 4 * KV_WIDTH), F32),
                   jax.ShapeDtypeStruct((M, 2 * KV_WIDTH), F32)],
        compiler_params=_cparams(("parallel", "arbitrary")),
        name="nsa_proj",
    )(x, w_in, cos, sin)


def rope_tables(pos):
    half = HEAD_DIM // 2
    inv = ROPE_THETA ** (-jnp.arange(half, dtype=F32) / half)
    ang = pos.astype(F32)[:, None] * inv[None, :]
    cos = jnp.cos(ang)
    sin = jnp.sin(ang)
    return jnp.concatenate([cos, cos], -1), jnp.concatenate([-sin, sin], -1)


def _compress_kernel(pt_ref, *refs):
    npg = PAGES_PER_STEP
    page_refs = refs[:npg]
    wcat_ref, pe_ref, w2_ref, o_ref, carry_ref = refs[npg:]
    j = pl.program_id(1)
    nch = 2 * KV_HEADS
    nsub = PAGE_SIZE // CMP_STRIDE
    R = npg * nsub * nch

    @pl.when(j == 0)
    def _():
        carry_ref[...] = jnp.zeros_like(carry_ref)

    def rows_at(l):
        return jnp.concatenate([page_refs[p][0, :, l].reshape(nsub * nch, HEAD_DIM) for p in range(npg)],
                               axis=0).astype(BF16)

    acc = jnp.zeros((R, 4 * HEAD_DIM), F32)
    for l2 in range(CMP_STRIDE // 2):
        x = jnp.concatenate([rows_at(2 * l2), rows_at(2 * l2 + 1)], axis=1)
        acc = acc + _dot(x, wcat_ref[l2])
    outs = []
    for t in range(2):
        a = acc[:, 2 * t * HEAD_DIM:(2 * t + 1) * HEAD_DIM]
        b = acc[:, (2 * t + 1) * HEAD_DIM:2 * (t + 1) * HEAD_DIM]
        a_shift = jnp.concatenate([carry_ref[t], a[:R - nch]], axis=0)
        carry_ref[t] = a[R - nch:]
        h = a_shift + b + pe_ref[t:t + 1, :]
        outs.append(_dot(jax.nn.gelu(h).astype(BF16), w2_ref[t]))
    is_k = lax.broadcasted_iota(jnp.int32, (R, HEAD_DIM), 0) % nch < KV_HEADS
    o_ref[0] = jnp.where(is_k, outs[0], outs[1])


def _compress_pe_kernel(w1_ref, pe_ref, o_ref):
    for t in range(2):
        acc = jnp.zeros((HEAD_DIM, HEAD_DIM), F32)
        for l in range(CMP_BLOCK):
            acc = acc + w1_ref[t, l] * pe_ref[t, l]
        o_ref[t:t + 1, :] = jnp.sum(acc, axis=0, keepdims=True)


def compress_pe_term(cmp_w1, cmp_pe):
    return pl.pallas_call(
        _compress_pe_kernel,
        out_shape=jax.ShapeDtypeStruct((2, HEAD_DIM), F32),
        compiler_params=_cparams(()),
        name="nsa_compress_pe",
    )(cmp_w1, cmp_pe[..., None])


def compress(pages, page_ids, n_batch, wcat, pe, w2):
    npg = PAGES_PER_STEP
    nch = 2 * KV_HEADS
    nsub = PAGE_SIZE // CMP_STRIDE
    ppb = page_ids.shape[0] // n_batch
    steps = ppb // npg
    R = npg * nsub * nch

    def page_spec(p):
        return pl.BlockSpec((1, nsub, CMP_STRIDE, nch, HEAD_DIM),
                            lambda b, j, pt: (pt[b * ppb + j * npg + p], 0, 0, 0, 0))

    const = lambda shape: pl.BlockSpec(shape, lambda b, j, pt: (0,) * len(shape))
    out = pl.pallas_call(
        _compress_kernel,
        grid_spec=pltpu.PrefetchScalarGridSpec(
            num_scalar_prefetch=1,
            grid=(n_batch, steps),
            in_specs=[page_spec(p) for p in range(npg)]
            + [const(wcat.shape), const(pe.shape), const(w2.shape)],
            out_specs=pl.BlockSpec((1, R, HEAD_DIM), lambda b, j, pt: (b, j, 0)),
            scratch_shapes=[pltpu.VMEM((2, nch, HEAD_DIM), F32)]),
        out_shape=jax.ShapeDtypeStruct((n_batch, steps * R, HEAD_DIM), F32),
        compiler_params=_cparams(("parallel", "arbitrary")),
        name="nsa_compress",
    )(page_ids, *([pages] * npg), wcat, pe, w2)
    return out.reshape(n_batch, steps * npg * nsub, nch * HEAD_DIM)


def _softmax_rows(s, mask):
    sm = jnp.where(mask, s, NEG)
    m = jnp.max(sm, axis=-1, keepdims=True)
    p = jnp.where(mask, jnp.exp2(sm - m), 0.0)
    return p * (1.0 / jnp.maximum(jnp.sum(p, axis=-1, keepdims=True), 1e-30))


def _online_step(s, mask, v, m, l, acc):
    sm = jnp.where(mask, s, NEG)
    m_new = jnp.maximum(m, jnp.max(sm, axis=-1, keepdims=True))
    p = jnp.where(mask, jnp.exp2(sm - m_new), 0.0)
    alpha = jnp.exp2(m - m_new)
    l = alpha * l + jnp.sum(p, axis=-1, keepdims=True)
    acc = alpha * acc + _dot(p.astype(BF16), v)
    return m_new, l, acc


def _mask_bias(mask):
    return jnp.where(mask, 0.0, NEG).astype(F32)


def _online_step_biased(s, bias, v, m, l, acc):
    R, C = s.shape
    T = bias.shape[0]
    sm = (s.reshape(R // T, T, C) + bias[None]).reshape(R, C)
    m_new = jnp.maximum(m, jnp.max(sm, axis=-1, keepdims=True))
    p = jnp.exp2(sm - m_new)
    alpha = jnp.exp2(m - m_new)
    l = alpha * l + jnp.sum(p, axis=-1, keepdims=True)
    acc = alpha * acc + _dot(p.astype(BF16), v)
    return m_new, l, acc


def _pool_matrix(n_rows, n_cols, transposed):
    shape = (n_cols, n_rows) if transposed else (n_rows, n_cols)
    n_ax, j_ax = (1, 0) if transposed else (0, 1)
    d = lax.broadcasted_iota(jnp.int32, shape, n_ax) - 4 * lax.broadcasted_iota(jnp.int32, shape, j_ax)
    return jnp.where((d == 0) | (d == 4), 1.0, jnp.where((d > 0) & (d < 4), 2.0, 0.0)).astype(F32)


TQ = 256
SEL_CHUNK = 512


def _nsa_prompt_kernel(q_ref, kc_ref, vc_ref, ks_ref, vs_ref, kw_ref, vw_ref, gate_ref, o_ref):
    qi = pl.program_id(2)
    t0 = qi * TQ
    R = HPG * TQ
    q = q_ref[...]
    qs = jnp.concatenate([q[:, h * HEAD_DIM:(h + 1) * HEAD_DIM] for h in range(HPG)], axis=0)
    t_row = t0 + lax.broadcasted_iota(jnp.int32, (TQ, 1), 0)
    t_rows = jnp.concatenate([t_row] * HPG, axis=0)

    nc = kc_ref.shape[1]
    s = _dot_nt(qs, kc_ref[0].astype(BF16))
    n_idx = lax.broadcasted_iota(jnp.int32, (1, nc), 1)
    cmask = (n_idx >= 1) & (CMP_STRIDE * n_idx + (CMP_STRIDE - 1) <= t_rows)
    p_cmp = _softmax_rows(s, cmask)
    o_cmp = _dot(p_cmp.astype(BF16), vc_ref[0].astype(BF16))
    pg = p_cmp[0:TQ]
    for h in range(1, HPG):
        pg = pg + p_cmp[h * TQ:(h + 1) * TQ]

    ns = ks_ref.shape[1] // SEL_BLOCK
    p_slc = _dot_nt(_pool_matrix(nc, ns, True), pg, precision=lax.Precision.HIGHEST)
    blk = lax.broadcasted_iota(jnp.int32, (ns, TQ), 0)
    cur = (t0 + lax.broadcasted_iota(jnp.int32, (ns, TQ), 1)) // SEL_BLOCK
    avail = blk <= cur
    forced = (blk == 0) | (blk == cur) | (blk == cur - 1)
    score = jnp.where(avail, p_slc + jnp.where(forced, FORCE_BONUS, 0.0), -jnp.inf)
    rank = jnp.zeros((ns, TQ), jnp.int32)
    for jj in range(ns):
        row = score[jj:jj + 1, :]
        beats = (row > score) | ((row == score) & (blk > jj))
        rank = rank + beats.astype(jnp.int32)
    sel_t = jnp.where((rank < SEL_TOPK) & avail, 1.0, 0.0).astype(F32)
    sel_t = jnp.concatenate([sel_t, jnp.zeros((LANES - ns, TQ), F32)], axis=0)
    sel = sel_t.T.astype(BF16)

    def sel_body(c, carry):
        k0 = pl.multiple_of(c * SEL_CHUNK, SEL_CHUNK)
        key = k0 + lax.broadcasted_iota(jnp.int32, (LANES, SEL_CHUNK), 1)
        expand = (lax.broadcasted_iota(jnp.int32, (LANES, SEL_CHUNK), 0) == key // SEL_BLOCK)
        picked = _dot(sel, expand.astype(BF16)) > 0.5
        kpos = k0 + lax.broadcasted_iota(jnp.int32, (1, SEL_CHUNK), 1)
        bias = _mask_bias(picked & (kpos <= t_row))
        k = ks_ref[0, pl.ds(k0, SEL_CHUNK), :].astype(BF16)
        v = vs_ref[0, pl.ds(k0, SEL_CHUNK), :].astype(BF16)
        return _online_step_biased(_dot_nt(qs, k), bias, v, *carry)

    init = (jnp.full((R, 1), NEG, F32), jnp.zeros((R, 1), F32), jnp.zeros((R, HEAD_DIM), F32))
    n_chunks = (t0 + TQ + SEL_CHUNK - 1) // SEL_CHUNK
    _, l, acc = lax.fori_loop(0, n_chunks, sel_body, init)
    o_sel = acc * (1.0 / l)

    band = WINDOW + TQ
    start = pl.multiple_of(jnp.maximum(t0 - WINDOW, 0), TQ)
    kpos = start + lax.broadcasted_iota(jnp.int32, (1, band), 1)
    bias = _mask_bias((kpos <= t_row) & (kpos > t_row - WINDOW))
    s = _dot_nt(qs, kw_ref[0, pl.ds(start, band), :].astype(BF16))
    _, l, acc = _online_step_biased(s, bias, vw_ref[0, pl.ds(start, band), :].astype(BF16), *init)
    o_win = acc * (1.0 / l)

    gates = gate_ref[0]
    for h in range(HPG):
        rows = slice(h * TQ, (h + 1) * TQ)
        o = (gates[:, h:h + 1] * o_cmp[rows] + gates[:, HPG + h:HPG + h + 1] * o_sel[rows]
             + gates[:, 2 * HPG + h:2 * HPG + h + 1] * o_win[rows])
        o_ref[:, h * HEAD_DIM:(h + 1) * HEAD_DIM] = o.astype(o_ref.dtype)


def nsa_prompt_attend(q, cmp, rows, win, gates_g, B, T):
    nq = T // TQ
    kv = lambda col0: pl.BlockSpec((1, T, HEAD_DIM), lambda b, g, i: (b, 0, col0 + g))
    cmp_spec = lambda col0: pl.BlockSpec((1, cmp.shape[1], HEAD_DIM), lambda b, g, i: (b, 0, col0 + g))
    return pl.pallas_call(
        _nsa_prompt_kernel,
        grid=(B, KV_HEADS, nq),
        in_specs=[pl.BlockSpec((TQ, HPG * HEAD_DIM), lambda b, g, i: (b * nq + i, g)),
                  cmp_spec(0), cmp_spec(KV_HEADS),
                  kv(2 * KV_HEADS), kv(3 * KV_HEADS), kv(0), kv(KV_HEADS),
                  pl.BlockSpec((1, TQ, LANES), lambda b, g, i: (g, b * nq + i, 0))],
        out_specs=pl.BlockSpec((TQ, HPG * HEAD_DIM), lambda b, g, i: (b * nq + i, g)),
        out_shape=jax.ShapeDtypeStruct((B * T, N_HEADS * HEAD_DIM), BF16),
        compiler_params=_cparams(("parallel", "parallel", "arbitrary")),
        name="nsa_prompt_attend",
    )(q, cmp, cmp, rows, rows, win, win, gates_g)


def _nsa_sample_select_kernel(q_ref, cmp_ref, ocmp_ref, sel_ref, *, past_len, ns_pad):
    T = q_ref.shape[1]
    R = HPG * T
    nc = cmp_ref.shape[1]
    q = q_ref[0]
    t_pos = past_len + lax.broadcasted_iota(jnp.int32, (T, 1), 0)
    t_rows = jnp.concatenate([t_pos] * HPG, axis=0)
    n_idx = lax.broadcasted_iota(jnp.int32, (1, nc), 1)
    cmask = (n_idx >= 1) & (CMP_STRIDE * n_idx + (CMP_STRIDE - 1) <= t_rows)
    pool = _pool_matrix(nc, ns_pad, False)
    pgs = []
    for g in range(KV_HEADS):
        qs = jnp.concatenate(
            [q[:, (g * HPG + h) * HEAD_DIM:(g * HPG + h + 1) * HEAD_DIM] for h in range(HPG)],
            axis=0).astype(BF16)
        kc = cmp_ref[0, :, g * HEAD_DIM:(g + 1) * HEAD_DIM].astype(BF16)
        vc = cmp_ref[0, :, (KV_HEADS + g) * HEAD_DIM:(KV_HEADS + g + 1) * HEAD_DIM].astype(BF16)
        p = _softmax_rows(_dot_nt(qs, kc), cmask)
        ocmp_ref[0, g] = _dot(p.astype(BF16), vc)
        pg = p[0:T]
        for h in range(1, HPG):
            pg = pg + p[h * T:(h + 1) * T]
        pgs.append(pg)
    pg = jnp.concatenate(pgs, axis=0)
    p_slc = jnp.dot(pg, pool, preferred_element_type=F32, precision=lax.Precision.HIGHEST)
    blk = lax.broadcasted_iota(jnp.int32, (KV_HEADS * T, ns_pad), 1)
    cur = jnp.concatenate([t_pos] * KV_HEADS, axis=0) // SEL_BLOCK
    avail = blk <= cur
    forced = (blk == 0) | (blk == cur) | (blk == cur - 1)
    score = jnp.where(avail, p_slc + jnp.where(forced, FORCE_BONUS, 0.0), -jnp.inf)
    sel = jnp.zeros(score.shape, F32)
    blk_f = blk.astype(F32)
    for _ in range(SEL_TOPK):
        m = jnp.max(score, axis=-1, keepdims=True)
        first = jnp.min(jnp.where(score == m, blk_f, float(ns_pad)), axis=-1, keepdims=True)
        pick = blk_f == first
        sel = jnp.where(pick & (m > -jnp.inf), 1.0, sel)
        score = jnp.where(pick, -jnp.inf, score)
    sel_ref[0] = sel


def _nsa_sample_attend_kernel(pt_ref, *refs, past_len, ns_pad):
    npg = PAGES_PER_STEP
    page_refs = refs[:npg]
    (q_ref, sel_ref, new_ref, band_ref, ocmp_ref, gate_ref, o_ref, m_ref, l_ref, acc_ref) = refs[npg:]
    c = pl.program_id(1)
    T = q_ref.shape[1]
    R = HPG * T
    chunk = npg * PAGE_SIZE
    q = q_ref[0]
    t_pos = past_len + lax.broadcasted_iota(jnp.int32, (T, 1), 0)
    t_rows = jnp.concatenate([t_pos] * HPG, axis=0)

    @pl.when(c == 0)
    def _():
        m_ref[...] = jnp.full(m_ref.shape, NEG, F32)
        l_ref[...] = jnp.zeros_like(l_ref)
        acc_ref[...] = jnp.zeros_like(acc_ref)

    def qs_of(g):
        return jnp.concatenate(
            [q[:, (g * HPG + h) * HEAD_DIM:(g * HPG + h + 1) * HEAD_DIM] for h in range(HPG)],
            axis=0).astype(BF16)

    def picked_masks(first_key, n_keys):
        key = first_key + lax.broadcasted_iota(jnp.int32, (ns_pad, n_keys), 1)
        expand = lax.broadcasted_iota(jnp.int32, (ns_pad, n_keys), 0) == key // SEL_BLOCK
        hit = _dot(sel_ref[0].astype(BF16), expand.astype(BF16))
        kpos = first_key + lax.broadcasted_iota(jnp.int32, (1, n_keys), 1)
        out = []
        for g in range(KV_HEADS):
            m1 = (hit[g * T:(g + 1) * T] > 0.5) & (kpos <= t_pos)
            out.append(jnp.concatenate([m1] * HPG, axis=0))
        return out

    def update(g, s, mask, v):
        m, l, acc = _online_step(s, mask, v, m_ref[g][:, 0:1], l_ref[g][:, 0:1], acc_ref[g])
        m_ref[g] = jnp.broadcast_to(m, (R, LANES))
        l_ref[g] = jnp.broadcast_to(l, (R, LANES))
        acc_ref[g] = acc

    masks = picked_masks(c * chunk, chunk)
    for g in range(KV_HEADS):
        qs = qs_of(g)
        k = jnp.concatenate([page_refs[p][0, :, g, :] for p in range(npg)], axis=0).astype(BF16)
        v = jnp.concatenate([page_refs[p][0, :, KV_HEADS + g, :] for p in range(npg)], axis=0).astype(BF16)
        update(g, _dot_nt(qs, k), masks[g], v)

    @pl.when(c == pl.num_programs(1) - 1)
    def _():
        n_new = new_ref.shape[1]
        n_band = band_ref.shape[1]
        gates = gate_ref[0]
        new_masks = picked_masks(past_len, n_new)
        for g in range(KV_HEADS):
            qs = qs_of(g)
            k = new_ref[0, :, g * HEAD_DIM:(g + 1) * HEAD_DIM].astype(BF16)
            v = new_ref[0, :, KV_WIDTH + g * HEAD_DIM:KV_WIDTH + (g + 1) * HEAD_DIM].astype(BF16)
            update(g, _dot_nt(qs, k), new_masks[g], v)
            o_sel = acc_ref[g] * (1.0 / jnp.maximum(l_ref[g][:, 0:1], 1e-30))
            kpos = (past_len - WINDOW) + lax.broadcasted_iota(jnp.int32, (1, n_band), 1)
            wmask = (kpos <= t_rows) & (kpos > t_rows - WINDOW)
            kw = band_ref[0, :, g * HEAD_DIM:(g + 1) * HEAD_DIM].astype(BF16)
            vw = band_ref[0, :, KV_WIDTH + g * HEAD_DIM:KV_WIDTH + (g + 1) * HEAD_DIM].astype(BF16)
            p_win = _softmax_rows(_dot_nt(qs, kw), wmask)
            o_win = _dot(p_win.astype(BF16), vw)
            o_cmp = ocmp_ref[0, g]
            for h in range(HPG):
                rows = slice(h * T, (h + 1) * T)
                col = g * HPG + h
                o = (gates[:, col:col + 1] * o_cmp[rows]
                     + gates[:, N_HEADS + col:N_HEADS + col + 1] * o_sel[rows]
                     + gates[:, 2 * N_HEADS + col:2 * N_HEADS + col + 1] * o_win[rows])
                o_ref[0, :, col * HEAD_DIM:(col + 1) * HEAD_DIM] = o


def nsa_sample_attend(q, cmp, pages, page_ids, new_rows, band, gates, past_len):
    B, T, _ = q.shape
    npg = PAGES_PER_STEP
    ppb = page_ids.shape[0] // B
    steps = ppb // npg
    ns = (past_len + T + SEL_BLOCK - 1) // SEL_BLOCK
    ns_pad = -(-ns // LANES) * LANES
    o_cmp, sel = pl.pallas_call(
        functools.partial(_nsa_sample_select_kernel, past_len=past_len, ns_pad=ns_pad),
        grid=(B,),
        in_specs=[pl.BlockSpec((1, T, q.shape[2]), lambda b: (b, 0, 0)),
                  pl.BlockSpec((1,) + cmp.shape[1:], lambda b: (b, 0, 0))],
        out_specs=[pl.BlockSpec((1, KV_HEADS, HPG * T, HEAD_DIM), lambda b: (b, 0, 0, 0)),
                   pl.BlockSpec((1, KV_HEADS * T, ns_pad), lambda b: (b, 0, 0))],
        out_shape=[jax.ShapeDtypeStruct((B, KV_HEADS, HPG * T, HEAD_DIM), F32),
                   jax.ShapeDtypeStruct((B, KV_HEADS * T, ns_pad), F32)],
        compiler_params=_cparams(("parallel",)),
        name="nsa_sample_select",
    )(q, cmp)

    def page_spec(p):
        return pl.BlockSpec((1, PAGE_SIZE, 2 * KV_HEADS, HEAD_DIM),
                            lambda b, c, pt: (pt[b * ppb + c * npg + p], 0, 1, 0))

    per_b = lambda a: pl.BlockSpec((1,) + a.shape[1:], lambda b, c, pt: (b,) + (0,) * (a.ndim - 1))
    R = HPG * T
    return pl.pallas_call(
        functools.partial(_nsa_sample_attend_kernel, past_len=past_len, ns_pad=ns_pad),
        grid_spec=pltpu.PrefetchScalarGridSpec(
            num_scalar_prefetch=1,
            grid=(B, steps),
            in_specs=[page_spec(p) for p in range(npg)]
            + [per_b(q), per_b(sel), per_b(new_rows), per_b(band), per_b(o_cmp), per_b(gates)],
            out_specs=pl.BlockSpec((1, T, q.shape[2]), lambda b, c, pt: (b, 0, 0)),
            scratch_shapes=[pltpu.VMEM((KV_HEADS, R, LANES), F32),
                            pltpu.VMEM((KV_HEADS, R, LANES), F32),
                            pltpu.VMEM((KV_HEADS, R, HEAD_DIM), F32)]),
        out_shape=jax.ShapeDtypeStruct(q.shape, F32),
        compiler_params=_cparams(("parallel", "arbitrary")),
        name="nsa_sample_attend",
    )(page_ids, *([pages] * npg), q, sel, new_rows, band, o_cmp, gates)


def _conv_gate_kernel(b_ref, c_ref, h_ref, prev_ref, w_ref, o_ref, st_ref, carry_ref):
    i = pl.program_id(1)
    tt = b_ref.shape[0]

    @pl.when(i == 0)
    def _():
        carry_ref[...] = prev_ref[0]

    u = c_ref[...] * h_ref[...]
    row = lax.broadcasted_iota(jnp.int32, u.shape, 0)
    prev = carry_ref[...]
    u1 = jnp.where(row == 0, prev[7:8, :], pltpu.roll(u, 1, axis=0))
    u2 = jnp.where(row == 0, prev[6:7, :], jnp.where(row == 1, prev[7:8, :], pltpu.roll(u, 2, axis=0)))
    w = w_ref[...]
    conv = w[0:1, :] * u2 + w[1:2, :] * u1 + w[2:3, :] * u
    o_ref[...] = (b_ref[...] * conv).astype(o_ref.dtype)
    carry_ref[...] = u[tt - 8:, :]
    st_ref[0] = u[tt - 8:, :]


def conv_gate(proj, prev8, w_conv, B, T, *, tt, out_dtype):
    D = proj.shape[1] // 3
    nt = T // tt
    col = lambda k: pl.BlockSpec((tt, D), lambda b, i: (b * nt + i, k))
    return pl.pallas_call(
        _conv_gate_kernel,
        grid=(B, nt),
        in_specs=[col(0), col(1), col(2),
                  pl.BlockSpec((1, 8, D), lambda b, i: (b, 0, 0)),
                  pl.BlockSpec((8, D), lambda b, i: (0, 0))],
        out_specs=[pl.BlockSpec((tt, D), lambda b, i: (b * nt + i, 0)),
                   pl.BlockSpec((1, 8, D), lambda b, i: (b, 0, 0))],
        out_shape=[jax.ShapeDtypeStruct((B * T, D), out_dtype), jax.ShapeDtypeStruct((B, 8, D), F32)],
        scratch_shapes=[pltpu.VMEM((8, D), F32)],
        compiler_params=_cparams(("parallel", "arbitrary")),
        name="conv_gate",
    )(proj, proj, proj, prev8, w_conv)


def _cmlp_mix_kernel(u_ref, v_ref, g_ref, b_ref, ws_ref, bs_ref, o_ref, vo_ref):
    c = u_ref.shape[0]
    v = _ln(v_ref[...], g_ref[...], b_ref[...])
    vo_ref[...] = v
    vb = v.astype(BF16)
    dg = v.shape[1] // SG_GROUPS
    tril = lax.broadcasted_iota(jnp.int32, (c, c), 0) >= lax.broadcasted_iota(jnp.int32, (c, c), 1)
    for g in range(SG_GROUPS):
        ws = jnp.where(tril, ws_ref[g], 0.0).astype(BF16)
        mixed = _dot(ws, vb[:, g * dg:(g + 1) * dg]) + bs_ref[g]
        o_ref[:, g * dg:(g + 1) * dg] = (u_ref[:, g * dg:(g + 1) * dg] * mixed).astype(o_ref.dtype)


def cmlp_mix(proj, ln_g, ln_b, w_s, b_s, *, c, out_dtype):
    M = proj.shape[0]
    D = proj.shape[1] // 2
    const = lambda a: pl.BlockSpec(a.shape, lambda i: (0,) * a.ndim)
    ln_g = ln_g.reshape(1, D)
    ln_b = ln_b.reshape(1, D)
    return pl.pallas_call(
        _cmlp_mix_kernel,
        grid=(M // c,),
        in_specs=[pl.BlockSpec((c, D), lambda i: (i, 0)), pl.BlockSpec((c, D), lambda i: (i, 1)),
                  const(ln_g), const(ln_b), const(w_s), const(b_s)],
        out_specs=[pl.BlockSpec((c, D), lambda i: (i, 0)), pl.BlockSpec((c, D), lambda i: (i, 0))],
        out_shape=[jax.ShapeDtypeStruct((M, D), out_dtype), jax.ShapeDtypeStruct((M, D), F32)],
        compiler_params=_cparams(("parallel",)),
        name="cmlp_mix",
    )(proj, proj, ln_g, ln_b, w_s, b_s)


def _router_kernel(x_ref, w_ref, o_ref):
    logits = jnp.dot(x_ref[...], w_ref[...], preferred_element_type=F32, precision=lax.Precision.HIGHEST)
    lane = lax.broadcasted_iota(jnp.int32, logits.shape, 1)
    lane_f = lane.astype(F32)
    lg = jnp.where(lane < N_EXPERTS, logits, -jnp.inf)
    m1 = jnp.max(lg, axis=-1, keepdims=True)
    i1 = jnp.min(jnp.where(lg == m1, lane_f, float(LANES)), axis=-1, keepdims=True)
    lg2 = jnp.where(lane_f == i1, -jnp.inf, lg)
    m2 = jnp.max(lg2, axis=-1, keepdims=True)
    i2 = jnp.min(jnp.where(lg2 == m2, lane_f, float(LANES)), axis=-1, keepdims=True)
    e = jnp.exp(m2 - m1)
    den = 1.0 + e
    g1 = 1.0 / den
    g2 = e / den
    o_ref[...] = jnp.where(lane == 0, i1, jnp.where(lane == 1, i2,
                                                    jnp.where(lane == 2, g1, jnp.where(lane == 3, g2, 0.0))))


def router_top2(x, w_router, *, tm):
    M, K = x.shape
    w = jnp.pad(w_router, ((0, 0), (0, LANES - N_EXPERTS)))
    return pl.pallas_call(
        _router_kernel,
        grid=(M // tm,),
        in_specs=[pl.BlockSpec((tm, K), lambda i: (i, 0)), pl.BlockSpec((K, LANES), lambda i: (0, 0))],
        out_specs=pl.BlockSpec((tm, LANES), lambda i: (i, 0)),
        out_shape=jax.ShapeDtypeStruct((M, LANES), F32),
        compiler_params=_cparams(("parallel",)),
        name="moe_router",
    )(x, w)


def _gmm_swiglu_kernel(be_ref, nv_ref, new_ref, x_ref, wg_ref, wu_ref, o_ref, wgb_ref, wub_ref):
    r = pl.program_id(1)

    @pl.when(r < nv_ref[0])
    def _():
        @pl.when(new_ref[r] == 1)
        def _():
            wgb_ref[...] = wg_ref[0, 0].astype(BF16)
            wub_ref[...] = wu_ref[0, 0].astype(BF16)

        x = x_ref[...].astype(BF16)
        g = _dot(x, wgb_ref[...])
        u = _dot(x, wub_ref[...])
        o_ref[...] = (jax.nn.silu(g) * u).astype(o_ref.dtype)


def gmm_swiglu(xs, w_gu, block_e, n_valid, new_expert, *, rb, tn, layer):
    P, K = xs.shape
    F = w_gu.shape[3] // 2
    nf = F // tn
    row = lambda r, nv: jnp.minimum(r, nv[0] - 1)
    return pl.pallas_call(
        _gmm_swiglu_kernel,
        grid_spec=pltpu.PrefetchScalarGridSpec(
            num_scalar_prefetch=3,
            grid=(nf, P // rb),
            in_specs=[pl.BlockSpec((rb, K), lambda n, r, be, nv, nw: (row(r, nv), 0)),
                      pl.BlockSpec((1, 1, K, tn), lambda n, r, be, nv, nw: (layer, be[row(r, nv)], 0, n)),
                      pl.BlockSpec((1, 1, K, tn), lambda n, r, be, nv, nw: (layer, be[row(r, nv)], 0, n + nf))],
            out_specs=pl.BlockSpec((rb, tn), lambda n, r, be, nv, nw: (row(r, nv), n)),
            scratch_shapes=[pltpu.VMEM((K, tn), BF16), pltpu.VMEM((K, tn), BF16)]),
        out_shape=jax.ShapeDtypeStruct((P, F), BF16),
        compiler_params=_cparams(("arbitrary", "arbitrary")),
        name="moe_gmm_swiglu",
    )(block_e, n_valid, new_expert, xs, w_gu, w_gu)


_IT_FIRST, _IT_LAST, _IT_NEW_W, _IT_SKIP = 1, 2, 4, 8


def _gmm_down_kernel(blk_ref, k_ref, e_ref, slot_ref, oblk_ref, flag_ref, h_ref, w_ref, o_ref, wb_ref, acc_ref):
    w = pl.program_id(0)
    f = flag_ref[w]

    @pl.when((f & _IT_SKIP) == 0)
    def _():
        @pl.when((f & _IT_NEW_W) != 0)
        def _():
            wb_ref[...] = w_ref[0, 0].astype(BF16)

        slot = slot_ref[w]
        part = _dot(h_ref[...], wb_ref[...])

        @pl.when((f & _IT_FIRST) != 0)
        def _():
            acc_ref[slot] = part

        @pl.when((f & _IT_FIRST) == 0)
        def _():
            acc_ref[slot] += part

        @pl.when((f & _IT_LAST) != 0)
        def _():
            o_ref[...] = acc_ref[slot]


def gmm_down(h, w_down, items, *, rb, tk, run, layer):
    P, F = h.shape
    D = w_down.shape[3]
    it_blk, it_k, it_e, it_slot, it_oblk, it_flag = items
    n_items = it_blk.shape[0]
    return pl.pallas_call(
        _gmm_down_kernel,
        grid_spec=pltpu.PrefetchScalarGridSpec(
            num_scalar_prefetch=6,
            grid=(n_items,),
            in_specs=[pl.BlockSpec((rb, tk), lambda w, blk, k, e, s, ob, fl: (blk[w], k[w])),
                      pl.BlockSpec((1, 1, tk, D), lambda w, blk, k, e, s, ob, fl: (layer, e[w], k[w], 0))],
            out_specs=pl.BlockSpec((rb, D), lambda w, blk, k, e, s, ob, fl: (ob[w], 0)),
            scratch_shapes=[pltpu.VMEM((tk, D), BF16), pltpu.VMEM((run, rb, D), F32)]),
        out_shape=jax.ShapeDtypeStruct((P, D), F32),
        compiler_params=_cparams(("arbitrary",)),
        name="moe_gmm_down",
    )(it_blk, it_k, it_e, it_slot, it_oblk, it_flag, h, w_down)


def _moe_combine_ln_kernel(res_ref, y0_ref, y1_ref, r_ref, g_ref, b_ref, o_ref, ob_ref):
    r = r_ref[...]
    f = r[:, 2:3] * y0_ref[...] + r[:, 3:4] * y1_ref[...]
    y = _ln(DEEPNORM_ALPHA * res_ref[...] + f, g_ref[...], b_ref[...])
    o_ref[...] = y
    ob_ref[...] = y.astype(BF16)


def moe_combine_ln(res, y0, y1, routed, gain, bias, *, tm):
    M, N = res.shape
    row = pl.BlockSpec((tm, N), lambda i: (i, 0))
    vec = pl.BlockSpec((1, N), lambda i: (0, 0))
    return pl.pallas_call(
        _moe_combine_ln_kernel,
        grid=(M // tm,),
        in_specs=[row, row, row, pl.BlockSpec((tm, LANES), lambda i: (i, 0)), vec, vec],
        out_specs=[row, row],
        out_shape=[jax.ShapeDtypeStruct((M, N), F32), jax.ShapeDtypeStruct((M, N), BF16)],
        compiler_params=_cparams(("parallel",)),
        name="moe_combine_ln",
    )(res, y0, y1, routed, gain.reshape(1, N), bias.reshape(1, N))


def moe_dispatch(top_e, *, rb, run, nk):
    A = top_e.size
    flat_e = top_e.reshape(A)
    experts = jnp.arange(N_EXPERTS, dtype=jnp.int32)
    onehot = (flat_e[:, None] == experts[None, :]).astype(jnp.int32)
    pos_in_e = jnp.sum((jnp.cumsum(onehot, axis=0) - onehot) * onehot, axis=1)
    counts = jnp.sum(onehot, axis=0)
    nblk_e = (counts + rb - 1) // rb
    bend_e = jnp.cumsum(nblk_e)
    bstart_e = bend_e - nblk_e
    dest = (bstart_e[flat_e] * rb + pos_in_e).astype(jnp.int32)
    n_blocks = -(-(A + N_EXPERTS * (rb - 1)) // rb)
    row_tok = jnp.zeros((n_blocks * rb,), jnp.int32).at[dest].set(jnp.arange(A, dtype=jnp.int32) // TOP_K)
    b = jnp.arange(n_blocks, dtype=jnp.int32)
    block_e = jnp.minimum(jnp.sum((bend_e[None, :] <= b[:, None]).astype(jnp.int32), axis=1), N_EXPERTS - 1)
    n_valid = bend_e[-1]
    new_expert = ((b == 0) | (block_e != jnp.roll(block_e, 1))).astype(jnp.int32)
    idx_in_e = b - bstart_e[block_e]
    slot = idx_in_e % run
    first = b - slot
    n_run = jnp.minimum(run, nblk_e[block_e] - (idx_in_e - slot))
    k = jnp.arange(nk, dtype=jnp.int32)[None, :]
    n_items = nk * n_blocks
    w_idx = nk * first[:, None] + k * n_run[:, None] + slot[:, None]
    w_idx = jnp.where((b < n_valid)[:, None], w_idx, n_items).reshape(-1)
    flag = (jnp.where(k == 0, _IT_FIRST, 0) | jnp.where(k == nk - 1, _IT_LAST, 0)
            | jnp.where(slot[:, None] == 0, _IT_NEW_W, 0))
    full = lambda v: jnp.broadcast_to(v, (n_blocks, nk)).reshape(-1).astype(jnp.int32)
    cols = [full(b[:, None]), full(k), full(block_e[:, None]), full(slot[:, None]),
            full(jnp.where(k == nk - 1, b[:, None], first[:, None])), full(flag)]
    items = [jnp.zeros((n_items,), jnp.int32).at[w_idx].set(c, mode="drop") for c in cols]
    w = jnp.arange(n_items, dtype=jnp.int32)
    last = nk * n_valid - 1
    items = [jnp.where(w <= last, c, c[last]) for c in items]
    items[5] = jnp.where(w <= last, items[5], _IT_SKIP)
    return dest, row_tok, block_e, n_valid.reshape(1), new_expert, items


MOE_RB = 384
MOE_RUN = 6
MOE_TN = 1024
MOE_TK = 1024


def moe_layer(xp, xs, w_router, w_gu, w_down, gain, bias, *, layer):
    Mp, Ms = xp.shape[0], xs.shape[0]
    rp = router_top2(xp, w_router, tm=1024)
    rs = router_top2(xs, w_router, tm=Ms)
    top_e = jnp.concatenate([rp[:, 0:2], rs[:, 0:2]], axis=0).astype(jnp.int32)
    nk = w_down.shape[2] // MOE_TK
    dest, row_tok, block_e, n_valid, new_expert, items = moe_dispatch(top_e, rb=MOE_RB, run=MOE_RUN, nk=nk)
    x_rows = jnp.concatenate([xp, xs], axis=0)[row_tok]
    h = gmm_swiglu(x_rows, w_gu, block_e, n_valid, new_expert, rb=MOE_RB, tn=MOE_TN, layer=layer)
    y = gmm_down(h, w_down, items, rb=MOE_RB, tk=MOE_TK, run=MOE_RUN, layer=layer)
    dest = dest.reshape(Mp + Ms, TOP_K)
    dp, ds = dest[:Mp], dest[Mp:]
    xp, xpb = moe_combine_ln(xp, y[dp[:, 0]], y[dp[:, 1]], rp, gain, bias, tm=512)
    xs, xsb = moe_combine_ln(xs, y[ds[:, 0]], y[ds[:, 1]], rs, gain, bias, tm=Ms)
    return xp, xpb, xs, xsb


def _compress_weights(cmp_w1, cmp_pe, cmp_w2):
    s = CMP_STRIDE
    wcat = jnp.concatenate([cmp_w1[0, :s], cmp_w1[0, s:], cmp_w1[1, :s], cmp_w1[1, s:]], axis=-1)
    wcat = wcat.reshape(s // 2, 2 * HEAD_DIM, 4 * HEAD_DIM).astype(BF16)
    return wcat, compress_pe_term(cmp_w1, cmp_pe), cmp_w2.astype(BF16)


def _gates_by_group(gates, M):
    g = gates[:, :3 * N_HEADS].reshape(M, 3, KV_HEADS, HPG).transpose(2, 0, 1, 3).reshape(KV_HEADS, M, 3 * HPG)
    return jnp.pad(g, ((0, 0), (0, 0), (0, LANES - 3 * HPG)))


def _gate_weights(w_in, layer):
    c0 = N_HEADS * HEAD_DIM + 6 * KV_WIDTH
    return jnp.pad(_stacked(w_in)[layer, :, c0:], ((0, 0), (0, LANES - 3 * N_HEADS)))


def _sub_block_view(rows):
    return rows.reshape(-1, PAGE_SIZE // CMP_STRIDE, CMP_STRIDE, 4 * KV_HEADS, HEAD_DIM)


def nsa_prompt_mixer(xb, w_in, cmp_w, cos, sin, B, T, *, tm, layer=0):
    M = B * T
    wcat, pe, w2 = cmp_w
    q, rows, win = nsa_proj(xb, w_in, cos, sin, tm=tm, q_dtype=BF16, layer=layer)
    gates = linear(xb, _gate_weights(w_in, layer), tm=tm, tn=LANES, act="sigmoid")
    page_ids = jnp.arange(M // PAGE_SIZE, dtype=jnp.int32)
    cmp = compress(_sub_block_view(rows), page_ids, B, wcat, pe, w2)
    o = nsa_prompt_attend(q, cmp, rows.reshape(B, T, 4 * KV_WIDTH), win.reshape(B, T, 2 * KV_WIDTH),
                          _gates_by_group(gates, M), B, T)
    wn = min(WINDOW, T)
    return (o, rows.reshape(B, T, 4, KV_HEADS, HEAD_DIM),
            win.reshape(B, T, 2, KV_HEADS, HEAD_DIM)[:, T - wn:])


def nsa_sample_mixer(xb, w_in, cmp_w, cos, sin, cache, page_ids, win_buf, B, T, *, layer=0):
    M = B * T
    past_len = page_ids.shape[0] // B * PAGE_SIZE
    wcat, pe, w2 = cmp_w
    q, rows, win = nsa_proj(xb, w_in, cos, sin, tm=M, q_dtype=F32, layer=layer)
    gates = linear(xb, _gate_weights(w_in, layer), tm=M, tn=LANES, act="sigmoid")
    cmp = compress(_sub_block_view(cache), page_ids, B, wcat, pe, w2)
    pages = cache.reshape(-1, PAGE_SIZE, 4 * KV_HEADS, HEAD_DIM)
    rows3 = rows.reshape(B, T, 4 * KV_WIDTH)
    new_rows = jnp.pad(rows3[:, :, 2 * KV_WIDTH:], ((0, 0), (0, LANES - T), (0, 0)))
    nbuf = win_buf.shape[1]
    band = jnp.concatenate([win_buf.reshape(B, nbuf, 2 * KV_WIDTH), win.reshape(B, T, 2 * KV_WIDTH)], axis=1)
    band_pad = jnp.pad(band, ((0, 0), (0, WINDOW + LANES - band.shape[1]), (0, 0)))
    o = nsa_sample_attend(q.reshape(B, T, -1), cmp, pages, page_ids, new_rows, band_pad,
                          gates.reshape(B, T, LANES), past_len)
    return (o.reshape(M, -1), rows.reshape(B, T, 4, KV_HEADS, HEAD_DIM),
            band[:, band.shape[1] - nbuf:].reshape(B, nbuf, 2, KV_HEADS, HEAD_DIM))


def conv_mixer(xb, prev, w_in, w_conv, B, T, *, tm, tt, layer=0):
    keep = CONV_WIDTH - 1
    proj = linear(xb, w_in, tm=tm, tn=512, layer=layer)
    prev8 = jnp.pad(prev, ((0, 0), (8 - keep, 0), (0, 0)))
    w8 = jnp.pad(w_conv, ((0, 8 - CONV_WIDTH), (0, 0)))
    gated, st = conv_gate(proj, prev8, w8, B, T, tt=tt, out_dtype=BF16 if tt % 16 == 0 else F32)
    return gated, st[:, 8 - keep:]


def chunk_mlp_mixer(xb, w_in, ln_g, ln_b, w_s, b_s, B, T, *, tm, layer=0):
    D = w_in.shape[-1] // 2
    c = min(T, CHUNK)
    proj = linear(xb, w_in, tm=tm, tn=512, act="gelu", layer=layer)
    cp = max(c, 16)
    if cp != c:
        proj = jnp.pad(proj.reshape(B * T // c, c, 2 * D), ((0, 0), (0, cp - c), (0, 0))).reshape(-1, 2 * D)
    ws = jnp.pad(w_s[:, :c, :c], ((0, 0), (0, cp - c), (0, cp - c)))
    bs = jnp.pad(b_s[:, :c, None], ((0, 0), (0, cp - c), (0, 0)))
    mixed, v = cmlp_mix(proj, ln_g, ln_b, ws, bs, c=cp, out_dtype=BF16 if c == cp else F32)
    if cp != c:
        mixed = mixed.reshape(-1, cp, D)[:, :c].reshape(B * T, D)
        v = v.reshape(-1, cp, D)[:, :c].reshape(B * T, D)
    return mixed, v.reshape(B, T, D)[:, T - c:]


def kernel(x_prompt, x_sample, cache_nsa_kv, state_nsa_win, state_conv, page_table, ln_gain, ln_bias, nsa_w_in, nsa_cmp_w1, nsa_cmp_pe, nsa_cmp_w2, nsa_w_out, conv_w_in, conv_w, conv_w_out, cmlp_w_in, cmlp_ln_gain, cmlp_ln_bias, cmlp_w_s, cmlp_b_s, cmlp_w_out, ffn_w_gu, ffn_w_down, moe_router, moe_w_gu, moe_w_down):
    B, T, D = x_prompt.shape
    SB, ST, _ = x_sample.shape
    Mp, Ms = B * T, SB * ST
    n_pool = cache_nsa_kv.shape[1]
    n_pages = page_table.shape[1]
    past_len = n_pages * PAGE_SIZE
    tm_p, tm_s = 1024, Ms

    xp = x_prompt.reshape(Mp, D)
    xs = x_sample.reshape(Ms, D)
    xpb, xsb = xp, xs

    pos_p = jnp.tile(jnp.arange(T, dtype=jnp.int32), B)
    pos_s = jnp.tile(past_len + jnp.arange(ST, dtype=jnp.int32), SB)
    cos_p, sin_p = rope_tables(pos_p)
    cos_s, sin_s = rope_tables(pos_s)
    nsa_w_out_b = nsa_w_out.astype(BF16)
    conv_w_out_b = conv_w_out.astype(BF16)
    cmlp_w_out_b = cmlp_w_out.astype(BF16)

    rows_p, rows_s, win_p, win_s = [], [], [], []
    conv_p, conv_s, v_p, v_s = [], [], [], []
    for i in range(DEPTH):
        m = i // 3
        kind = i % 3
        g0, b0 = ln_gain[i, 0], ln_bias[i, 0]
        if kind == 0:
            w = (nsa_w_in, _compress_weights(nsa_cmp_w1[m], nsa_cmp_pe[m], nsa_cmp_w2[m]))
            o, rows, win = nsa_prompt_mixer(xpb, *w, cos_p, sin_p, B, T, tm=tm_p, layer=m)
            xp, xpb = linear_res_ln(o, nsa_w_out_b, xp, g0, b0, tm=512, tk=D, layer=m)
            rows_p.append(rows)
            win_p.append(win)
            page_ids = (m * n_pool + page_table).reshape(-1).astype(jnp.int32)
            o, rows, win = nsa_sample_mixer(xsb, *w, cos_s, sin_s, cache_nsa_kv, page_ids, state_nsa_win[m],
                                            SB, ST, layer=m)
            xs, xsb = linear_res_ln(o, nsa_w_out_b, xs, g0, b0, tm=tm_s, tk=D, layer=m)
            rows_s.append(rows)
            win_s.append(win)
        elif kind == 1:
            w = (conv_w_in, conv_w[m])
            gated, st = conv_mixer(xpb, jnp.zeros((B, CONV_WIDTH - 1, D), F32), *w, B, T, tm=tm_p, tt=256,
                                   layer=m)
            xp, xpb = linear_res_ln(gated, conv_w_out_b, xp, g0, b0, tm=512, tk=D, layer=m)
            conv_p.append(st)
            gated, st = conv_mixer(xsb, state_conv[m], *w, SB, ST, tm=tm_s, tt=ST, layer=m)
            xs, xsb = linear_res_ln(gated, conv_w_out_b, xs, g0, b0, tm=tm_s, tk=D, layer=m)
            conv_s.append(st)
        else:
            w = (cmlp_w_in, cmlp_ln_gain[m], cmlp_ln_bias[m], cmlp_w_s[m], cmlp_b_s[m])
            mixed, v = chunk_mlp_mixer(xpb, *w, B, T, tm=tm_p, layer=m)
            xp, xpb = linear_res_ln(mixed, cmlp_w_out_b, xp, g0, b0, tm=512, tk=D, layer=m)
            v_p.append(v)
            mixed, v = chunk_mlp_mixer(xsb, *w, SB, ST, tm=tm_s, layer=m)
            xs, xsb = linear_res_ln(mixed, cmlp_w_out_b, xs, g0, b0, tm=tm_s, tk=D, layer=m)
            v_s.append(v)
        g1, b1 = ln_gain[i, 1], ln_bias[i, 1]
        f = i // 2
        if i % 2 == 0:
            h = linear_swiglu(xpb, ffn_w_gu, tm=tm_p, tn=512, layer=f)
            xp, xpb = linear_res_ln(h, ffn_w_down, xp, g1, b1, tm=1024, tk=512, layer=f)
            h = linear_swiglu(xsb, ffn_w_gu, tm=tm_s, tn=512, layer=f)
            xs, xsb = linear_res_ln(h, ffn_w_down, xs, g1, b1, tm=tm_s, tk=512, layer=f)
        else:
            xp, xpb, xs, xsb = moe_layer(xp, xs, moe_router[f], moe_w_gu, moe_w_down, g1, b1, layer=f)
    return (xp.reshape(B, T, D), xs.reshape(SB, ST, D),
            jnp.stack(rows_p), jnp.stack(rows_s), jnp.stack(win_p), jnp.stack(win_s),
            jnp.stack(conv_p), jnp.stack(conv_s), jnp.stack(v_p), jnp.stack(v_s))
```

```python
import functools

import jax
import jax.numpy as jnp
from jax import lax
from jax.experimental import pallas as pl
from jax.experimental.pallas import tpu as pltpu

D_MODEL = 2048
DEPTH = 4
PAGE_SIZE = 128
N_HEADS = 16
HEAD_DIM = 128
KV_HEADS = 4
HPG = N_HEADS // KV_HEADS
KV_WIDTH = KV_HEADS * HEAD_DIM
CMP_BLOCK = 32
CMP_STRIDE = 16
SEL_BLOCK = 64
SEL_TOPK = 16
WINDOW = 512
ROPE_THETA = 10000.0
FORCE_BONUS = 1e4
CONV_WIDTH = 3
CHUNK = 128
SG_GROUPS = 8
N_EXPERTS = 8
TOP_K = 2
LN_EPS = 1e-5
DEEPNORM_ALPHA = (2 * DEPTH) ** 0.25

LANES = 128
PAGES_PER_STEP = 8
NEG = -1e30
Q_SCALE = HEAD_DIM ** -0.5 * 1.4426950408889634
VMEM_LIMIT = 56 * 1024 * 1024

F32 = jnp.float32
BF16 = jnp.bfloat16


def _cparams(sem):
    return pltpu.CompilerParams(dimension_semantics=sem, vmem_limit_bytes=VMEM_LIMIT)


def _ln(y, g, b):
    mu = jnp.mean(y, axis=-1, keepdims=True)
    d = y - mu
    var = jnp.mean(d * d, axis=-1, keepdims=True)
    return d * lax.rsqrt(var + LN_EPS) * g + b


def _dot(a, b):
    return jnp.dot(a, b, preferred_element_type=F32)


def _dot_nt(a, b, precision=None):
    return lax.dot_general(a, b, (((1,), (1,)), ((), ())), preferred_element_type=F32,
                           precision=precision)


def _stacked(w):
    return w if w.ndim == 3 else w[None]


def _linear_kernel(x_ref, w_ref, o_ref, *, act):
    y = _dot(x_ref[...].astype(BF16), w_ref[0].astype(BF16))
    if act == "gelu":
        y = jax.nn.gelu(y)
    elif act == "sigmoid":
        y = jax.nn.sigmoid(y)
    o_ref[...] = y.astype(o_ref.dtype)


def linear(x, w, *, tm, tn, layer=0, act=None, out_dtype=F32):
    w = _stacked(w)
    M, K = x.shape
    N = w.shape[2]
    return pl.pallas_call(
        functools.partial(_linear_kernel, act=act),
        grid=(M // tm, N // tn),
        in_specs=[pl.BlockSpec((tm, K), lambda i, j: (i, 0)),
                  pl.BlockSpec((1, K, tn), lambda i, j: (layer, 0, j))],
        out_specs=pl.BlockSpec((tm, tn), lambda i, j: (i, j)),
        out_shape=jax.ShapeDtypeStruct((M, N), out_dtype),
        compiler_params=_cparams(("parallel", "arbitrary")),
        name="linear",
    )(x, w)


def _swiglu_kernel(x_ref, wg_ref, wu_ref, o_ref):
    x = x_ref[...].astype(BF16)
    g = _dot(x, wg_ref[0].astype(BF16))
    u = _dot(x, wu_ref[0].astype(BF16))
    o_ref[...] = (jax.nn.silu(g) * u).astype(o_ref.dtype)


def linear_swiglu(x, w_gu, *, tm, tn, layer=0):
    w_gu = _stacked(w_gu)
    M, K = x.shape
    F = w_gu.shape[2] // 2
    nf = F // tn
    return pl.pallas_call(
        _swiglu_kernel,
        grid=(M // tm, nf),
        in_specs=[pl.BlockSpec((tm, K), lambda i, j: (i, 0)),
                  pl.BlockSpec((1, K, tn), lambda i, j: (layer, 0, j)),
                  pl.BlockSpec((1, K, tn), lambda i, j: (layer, 0, j + nf))],
        out_specs=pl.BlockSpec((tm, tn), lambda i, j: (i, j)),
        out_shape=jax.ShapeDtypeStruct((M, F), BF16),
        compiler_params=_cparams(("parallel", "arbitrary")),
        name="linear_swiglu",
    )(x, w_gu, w_gu)


def _linear_res_ln_kernel(h_ref, w_ref, res_ref, g_ref, b_ref, o_ref, ob_ref, *acc):
    def finish(f):
        y = _ln(DEEPNORM_ALPHA * res_ref[...] + f, g_ref[...], b_ref[...])
        o_ref[...] = y
        ob_ref[...] = y.astype(BF16)

    part = _dot(h_ref[...].astype(BF16), w_ref[0].astype(BF16))
    if not acc:
        finish(part)
        return
    acc_ref, = acc
    k = pl.program_id(1)

    @pl.when(k == 0)
    def _():
        acc_ref[...] = part

    @pl.when(k > 0)
    def _():
        acc_ref[...] += part

    @pl.when(k == pl.num_programs(1) - 1)
    def _():
        finish(acc_ref[...])


def linear_res_ln(h, w, res, gain, bias, *, tm, tk, layer=0):
    w = _stacked(w)
    M, K = h.shape
    N = w.shape[2]
    row_mode = dict(pipeline_mode=pl.Buffered(1)) if K != tk else {}
    row = lambda: pl.BlockSpec((tm, N), lambda i, k: (i, 0), **row_mode)
    return pl.pallas_call(
        _linear_res_ln_kernel,
        grid=(M // tm, K // tk),
        in_specs=[pl.BlockSpec((tm, tk), lambda i, k: (i, k)),
                  pl.BlockSpec((1, tk, N), lambda i, k: (layer, k, 0)),
                  row(),
                  pl.BlockSpec((1, N), lambda i, k: (0, 0)),
                  pl.BlockSpec((1, N), lambda i, k: (0, 0))],
        out_specs=[row(), row()],
        out_shape=[jax.ShapeDtypeStruct((M, N), F32), jax.ShapeDtypeStruct((M, N), BF16)],
        scratch_shapes=[pltpu.VMEM((tm, N), F32)] if K != tk else [],
        compiler_params=_cparams(("parallel", "arbitrary")),
        name="linear_res_ln",
    )(h, w, res, gain.reshape(1, N), bias.reshape(1, N))


def _rope_tile(y, cos, sin):
    outs = []
    for c in range(y.shape[1] // HEAD_DIM):
        yc = y[:, c * HEAD_DIM:(c + 1) * HEAD_DIM]
        outs.append(yc * cos + pltpu.roll(yc, HEAD_DIM // 2, axis=1) * sin)
    return jnp.concatenate(outs, axis=1)


def _nsa_proj_kernel(x_ref, w_ref, cos_ref, sin_ref, q_ref, rows_ref, win_ref):
    j = pl.program_id(1)
    y = _dot(x_ref[...].astype(BF16), w_ref[0].astype(BF16))
    cos = cos_ref[...]
    sin = sin_ref[...]

    @pl.when(j < 4)
    def _():
        q_ref[...] = (_rope_tile(y, cos, sin) * Q_SCALE).astype(q_ref.dtype)

    @pl.when((j == 4) | (j == 6))
    def _():
        rows_ref[...] = _rope_tile(y, cos, sin)

    @pl.when((j == 5) | (j == 7))
    def _():
        rows_ref[...] = y

    @pl.when(j == 8)
    def _():
        win_ref[...] = _rope_tile(y, cos, sin)

    @pl.when(j == 9)
    def _():
        win_ref[...] = y


def nsa_proj(x, w_in, cos, sin, *, tm, q_dtype, layer=0):
    w_in = _stacked(w_in)
    M, K = x.shape
    tn = KV_WIDTH
    qd = N_HEADS * HEAD_DIM
    return pl.pallas_call(
        _nsa_proj_kernel,
        grid=(M // tm, 10),
        in_specs=[pl.BlockSpec((tm, K), lambda i, j: (i, 0)),
                  pl.BlockSpec((1, K, tn), lambda i, j: (layer, 0, j)),
                  pl.BlockSpec((tm, HEAD_DIM), lambda i, j: (i, 0)),
                  pl.BlockSpec((tm, HEAD_DIM), lambda i, j: (i, 0))],
        out_specs=[pl.BlockSpec((tm, tn), lambda i, j: (i, jnp.minimum(j, 3))),
                   pl.BlockSpec((tm, tn), lambda i, j: (i, jnp.clip(j - 4, 0, 3))),
                   pl.BlockSpec((tm, tn), lambda i, j: (i, jnp.clip(j - 8, 0, 1)))],
        out_shape=[jax.ShapeDtypeStruct((M, qd), q_dtype),
                   jax.ShapeDtypeStruct((M, 4 * KV_WIDTH), F32),
                   jax.ShapeDtypeStruct((M, 2 * KV_WIDTH), F32)],
        compiler_params=_cparams(("parallel", "arbitrary")),
        name="nsa_proj",
    )(x, w_in, cos, sin)


def rope_tables(pos):
    half = HEAD_DIM // 2
    inv = ROPE_THETA ** (-jnp.arange(half, dtype=F32) / half)
    ang = pos.astype(F32)[:, None] * inv[None, :]
    cos = jnp.cos(ang)
    sin = jnp.sin(ang)
    return jnp.concatenate([cos, cos], -1), jnp.concatenate([-sin, sin], -1)


def _compress_kernel(pt_ref, *refs):
    npg = PAGES_PER_STEP
    page_refs = refs[:npg]
    wcat_ref, pe_ref, w2_ref, o_ref, carry_ref = refs[npg:]
    j = pl.program_id(1)
    nch = 2 * KV_HEADS
    nsub = PAGE_SIZE // CMP_STRIDE
    R = npg * nsub * nch

    @pl.when(j == 0)
    def _():
        carry_ref[...] = jnp.zeros_like(carry_ref)

    def rows_at(l):
        return jnp.concatenate([page_refs[p][0, :, l].reshape(nsub * nch, HEAD_DIM) for p in range(npg)],
                               axis=0).astype(BF16)

    acc = jnp.zeros((R, 4 * HEAD_DIM), F32)
    for l2 in range(CMP_STRIDE // 2):
        x = jnp.concatenate([rows_at(2 * l2), rows_at(2 * l2 + 1)], axis=1)
        acc = acc + _dot(x, wcat_ref[l2])
    outs = []
    for t in range(2):
        a = acc[:, 2 * t * HEAD_DIM:(2 * t + 1) * HEAD_DIM]
        b = acc[:, (2 * t + 1) * HEAD_DIM:2 * (t + 1) * HEAD_DIM]
        a_shift = jnp.concatenate([carry_ref[t], a[:R - nch]], axis=0)
        carry_ref[t] = a[R - nch:]
        h = a_shift + b + pe_ref[t:t + 1, :]
        outs.append(_dot(jax.nn.gelu(h).astype(BF16), w2_ref[t]))
    is_k = lax.broadcasted_iota(jnp.int32, (R, HEAD_DIM), 0) % nch < KV_HEADS
    o_ref[0] = jnp.where(is_k, outs[0], outs[1])


def _compress_pe_kernel(w1_ref, pe_ref, o_ref):
    for t in range(2):
        acc = jnp.zeros((HEAD_DIM, HEAD_DIM), F32)
        for l in range(CMP_BLOCK):
            acc = acc + w1_ref[t, l] * pe_ref[t, l]
        o_ref[t:t + 1, :] = jnp.sum(acc, axis=0, keepdims=True)


def compress_pe_term(cmp_w1, cmp_pe):
    return pl.pallas_call(
        _compress_pe_kernel,
        out_shape=jax.ShapeDtypeStruct((2, HEAD_DIM), F32),
        compiler_params=_cparams(()),
        name="nsa_compress_pe",
    )(cmp_w1, cmp_pe[..., None])


def compress(pages, page_ids, n_batch, wcat, pe, w2):
    npg = PAGES_PER_STEP
    nch = 2 * KV_HEADS
    nsub = PAGE_SIZE // CMP_STRIDE
    ppb = page_ids.shape[0] // n_batch
    steps = ppb // npg
    R = npg * nsub * nch

    def page_spec(p):
        return pl.BlockSpec((1, nsub, CMP_STRIDE, nch, HEAD_DIM),
                            lambda b, j, pt: (pt[b * ppb + j * npg + p], 0, 0, 0, 0))

    const = lambda shape: pl.BlockSpec(shape, lambda b, j, pt: (0,) * len(shape))
    out = pl.pallas_call(
        _compress_kernel,
        grid_spec=pltpu.PrefetchScalarGridSpec(
            num_scalar_prefetch=1,
            grid=(n_batch, steps),
            in_specs=[page_spec(p) for p in range(npg)]
            + [const(wcat.shape), const(pe.shape), const(w2.shape)],
            out_specs=pl.BlockSpec((1, R, HEAD_DIM), lambda b, j, pt: (b, j, 0)),
            scratch_shapes=[pltpu.VMEM((2, nch, HEAD_DIM), F32)]),
        out_shape=jax.ShapeDtypeStruct((n_batch, steps * R, HEAD_DIM), F32),
        compiler_params=_cparams(("parallel", "arbitrary")),
        name="nsa_compress",
    )(page_ids, *([pages] * npg), wcat, pe, w2)
    return out.reshape(n_batch, steps * npg * nsub, nch * HEAD_DIM)


def _softmax_rows(s, mask):
    sm = jnp.where(mask, s, NEG)
    m = jnp.max(sm, axis=-1, keepdims=True)
    p = jnp.where(mask, jnp.exp2(sm - m), 0.0)
    return p * (1.0 / jnp.maximum(jnp.sum(p, axis=-1, keepdims=True), 1e-30))


def _online_step(s, mask, v, m, l, acc):
    sm = jnp.where(mask, s, NEG)
    m_new = jnp.maximum(m, jnp.max(sm, axis=-1, keepdims=True))
    p = jnp.where(mask, jnp.exp2(sm - m_new), 0.0)
    alpha = jnp.exp2(m - m_new)
    l = alpha * l + jnp.sum(p, axis=-1, keepdims=True)
    acc = alpha * acc + _dot(p.astype(BF16), v)
    return m_new, l, acc


def _mask_bias(mask):
    return jnp.where(mask, 0.0, NEG).astype(F32)


def _online_step_biased(s, bias, v, m, l, acc):
    R, C = s.shape
    T = bias.shape[0]
    sm = (s.reshape(R // T, T, C) + bias[None]).reshape(R, C)
    m_new = jnp.maximum(m, jnp.max(sm, axis=-1, keepdims=True))
    p = jnp.exp2(sm - m_new)
    alpha = jnp.exp2(m - m_new)
    l = alpha * l + jnp.sum(p, axis=-1, keepdims=True)
    acc = alpha * acc + _dot(p.astype(BF16), v)
    return m_new, l, acc


def _pool_matrix(n_rows, n_cols, transposed):
    shape = (n_cols, n_rows) if transposed else (n_rows, n_cols)
    n_ax, j_ax = (1, 0) if transposed else (0, 1)
    d = lax.broadcasted_iota(jnp.int32, shape, n_ax) - 4 * lax.broadcasted_iota(jnp.int32, shape, j_ax)
    return jnp.where((d == 0) | (d == 4), 1.0, jnp.where((d > 0) & (d < 4), 2.0, 0.0)).astype(F32)


TQ = 256
SEL_CHUNK = 512


def _nsa_prompt_kernel(q_ref, kc_ref, vc_ref, ks_ref, vs_ref, kw_ref, vw_ref, gate_ref, o_ref):
    qi = pl.program_id(2)
    t0 = qi * TQ
    R = HPG * TQ
    q = q_ref[...]
    qs = jnp.concatenate([q[:, h * HEAD_DIM:(h + 1) * HEAD_DIM] for h in range(HPG)], axis=0)
    t_row = t0 + lax.broadcasted_iota(jnp.int32, (TQ, 1), 0)
    t_rows = jnp.concatenate([t_row] * HPG, axis=0)

    nc = kc_ref.shape[1]
    s = _dot_nt(qs, kc_ref[0].astype(BF16))
    n_idx = lax.broadcasted_iota(jnp.int32, (1, nc), 1)
    cmask = (n_idx >= 1) & (CMP_STRIDE * n_idx + (CMP_STRIDE - 1) <= t_rows)
    p_cmp = _softmax_rows(s, cmask)
    o_cmp = _dot(p_cmp.astype(BF16), vc_ref[0].astype(BF16))
    pg = p_cmp[0:TQ]
    for h in range(1, HPG):
        pg = pg + p_cmp[h * TQ:(h + 1) * TQ]

    ns = ks_ref.shape[1] // SEL_BLOCK
    p_slc = _dot_nt(_pool_matrix(nc, ns, True), pg, precision=lax.Precision.HIGHEST)
    blk = lax.broadcasted_iota(jnp.int32, (ns, TQ), 0)
    cur = (t0 + lax.broadcasted_iota(jnp.int32, (ns, TQ), 1)) // SEL_BLOCK
    avail = blk <= cur
    forced = (blk == 0) | (blk == cur) | (blk == cur - 1)
    score = jnp.where(avail, p_slc + jnp.where(forced, FORCE_BONUS, 0.0), -jnp.inf)
    rank = jnp.zeros((ns, TQ), jnp.int32)
    for jj in range(ns):
        row = score[jj:jj + 1, :]
        beats = (row > score) | ((row == score) & (blk > jj))
        rank = rank + beats.astype(jnp.int32)
    sel_t = jnp.where((rank < SEL_TOPK) & avail, 1.0, 0.0).astype(F32)
    sel_t = jnp.concatenate([sel_t, jnp.zeros((LANES - ns, TQ), F32)], axis=0)
    sel = sel_t.T.astype(BF16)

    def sel_body(c, carry):
        k0 = pl.multiple_of(c * SEL_CHUNK, SEL_CHUNK)
        key = k0 + lax.broadcasted_iota(jnp.int32, (LANES, SEL_CHUNK), 1)
        expand = (lax.broadcasted_iota(jnp.int32, (LANES, SEL_CHUNK), 0) == key // SEL_BLOCK)
        picked = _dot(sel, expand.astype(BF16)) > 0.5
        kpos = k0 + lax.broadcasted_iota(jnp.int32, (1, SEL_CHUNK), 1)
        bias = _mask_bias(picked & (kpos <= t_row))
        k = ks_ref[0, pl.ds(k0, SEL_CHUNK), :].astype(BF16)
        v = vs_ref[0, pl.ds(k0, SEL_CHUNK), :].astype(BF16)
        return _online_step_biased(_dot_nt(qs, k), bias, v, *carry)

    init = (jnp.full((R, 1), NEG, F32), jnp.zeros((R, 1), F32), jnp.zeros((R, HEAD_DIM), F32))
    n_chunks = (t0 + TQ + SEL_CHUNK - 1) // SEL_CHUNK
    _, l, acc = lax.fori_loop(0, n_chunks, sel_body, init)
    o_sel = acc * (1.0 / l)

    band = WINDOW + TQ
    start = pl.multiple_of(jnp.maximum(t0 - WINDOW, 0), TQ)
    kpos = start + lax.broadcasted_iota(jnp.int32, (1, band), 1)
    bias = _mask_bias((kpos <= t_row) & (kpos > t_row - WINDOW))
    s = _dot_nt(qs, kw_ref[0, pl.ds(start, band), :].astype(BF16))
    _, l, acc = _online_step_biased(s, bias, vw_ref[0, pl.ds(start, band), :].astype(BF16), *init)
    o_win = acc * (1.0 / l)

    gates = gate_ref[0]
    for h in range(HPG):
        rows = slice(h * TQ, (h + 1) * TQ)
        o = (gates[:, h:h + 1] * o_cmp[rows] + gates[:, HPG + h:HPG + h + 1] * o_sel[rows]
             + gates[:, 2 * HPG + h:2 * HPG + h + 1] * o_win[rows])
        o_ref[:, h * HEAD_DIM:(h + 1) * HEAD_DIM] = o.astype(o_ref.dtype)


def nsa_prompt_attend(q, cmp, rows, win, gates_g, B, T):
    nq = T // TQ
    kv = lambda col0: pl.BlockSpec((1, T, HEAD_DIM), lambda b, g, i: (b, 0, col0 + g))
    cmp_spec = lambda col0: pl.BlockSpec((1, cmp.shape[1], HEAD_DIM), lambda b, g, i: (b, 0, col0 + g))
    return pl.pallas_call(
        _nsa_prompt_kernel,
        grid=(B, KV_HEADS, nq),
        in_specs=[pl.BlockSpec((TQ, HPG * HEAD_DIM), lambda b, g, i: (b * nq + i, g)),
                  cmp_spec(0), cmp_spec(KV_HEADS),
                  kv(2 * KV_HEADS), kv(3 * KV_HEADS), kv(0), kv(KV_HEADS),
                  pl.BlockSpec((1, TQ, LANES), lambda b, g, i: (g, b * nq + i, 0))],
        out_specs=pl.BlockSpec((TQ, HPG * HEAD_DIM), lambda b, g, i: (b * nq + i, g)),
        out_shape=jax.ShapeDtypeStruct((B * T, N_HEADS * HEAD_DIM), BF16),
        compiler_params=_cparams(("parallel", "parallel", "arbitrary")),
        name="nsa_prompt_attend",
    )(q, cmp, cmp, rows, rows, win, win, gates_g)


def _nsa_sample_select_kernel(q_ref, cmp_ref, ocmp_ref, sel_ref, *, past_len, ns_pad):
    T = q_ref.shape[1]
    R = HPG * T
    nc = cmp_ref.shape[1]
    q = q_ref[0]
    t_pos = past_len + lax.broadcasted_iota(jnp.int32, (T, 1), 0)
    t_rows = jnp.concatenate([t_pos] * HPG, axis=0)
    n_idx = lax.broadcasted_iota(jnp.int32, (1, nc), 1)
    cmask = (n_idx >= 1) & (CMP_STRIDE * n_idx + (CMP_STRIDE - 1) <= t_rows)
    pool = _pool_matrix(nc, ns_pad, False)
    pgs = []
    for g in range(KV_HEADS):
        qs = jnp.concatenate(
            [q[:, (g * HPG + h) * HEAD_DIM:(g * HPG + h + 1) * HEAD_DIM] for h in range(HPG)],
            axis=0).astype(BF16)
        kc = cmp_ref[0, :, g * HEAD_DIM:(g + 1) * HEAD_DIM].astype(BF16)
        vc = cmp_ref[0, :, (KV_HEADS + g) * HEAD_DIM:(KV_HEADS + g + 1) * HEAD_DIM].astype(BF16)
        p = _softmax_rows(_dot_nt(qs, kc), cmask)
        ocmp_ref[0, g] = _dot(p.astype(BF16), vc)
        pg = p[0:T]
        for h in range(1, HPG):
            pg = pg + p[h * T:(h + 1) * T]
        pgs.append(pg)
    pg = jnp.concatenate(pgs, axis=0)
    p_slc = jnp.dot(pg, pool, preferred_element_type=F32, precision=lax.Precision.HIGHEST)
    blk = lax.broadcasted_iota(jnp.int32, (KV_HEADS * T, ns_pad), 1)
    cur = jnp.concatenate([t_pos] * KV_HEADS, axis=0) // SEL_BLOCK
    avail = blk <= cur
    forced = (blk == 0) | (blk == cur) | (blk == cur - 1)
    score = jnp.where(avail, p_slc + jnp.where(forced, FORCE_BONUS, 0.0), -jnp.inf)
    sel = jnp.zeros(score.shape, F32)
    blk_f = blk.astype(F32)
    for _ in range(SEL_TOPK):
        m = jnp.max(score, axis=-1, keepdims=True)
        first = jnp.min(jnp.where(score == m, blk_f, float(ns_pad)), axis=-1, keepdims=True)
        pick = blk_f == first
        sel = jnp.where(pick & (m > -jnp.inf), 1.0, sel)
        score = jnp.where(pick, -jnp.inf, score)
    sel_ref[0] = sel


def _nsa_sample_attend_kernel(pt_ref, *refs, past_len, ns_pad):
    npg = PAGES_PER_STEP
    page_refs = refs[:npg]
    (q_ref, sel_ref, new_ref, band_ref, ocmp_ref, gate_ref, o_ref, m_ref, l_ref, acc_ref) = refs[npg:]
    c = pl.program_id(1)
    T = q_ref.shape[1]
    R = HPG * T
    chunk = npg * PAGE_SIZE
    q = q_ref[0]
    t_pos = past_len + lax.broadcasted_iota(jnp.int32, (T, 1), 0)
    t_rows = jnp.concatenate([t_pos] * HPG, axis=0)

    @pl.when(c == 0)
    def _():
        m_ref[...] = jnp.full(m_ref.shape, NEG, F32)
        l_ref[...] = jnp.zeros_like(l_ref)
        acc_ref[...] = jnp.zeros_like(acc_ref)

    def qs_of(g):
        return jnp.concatenate(
            [q[:, (g * HPG + h) * HEAD_DIM:(g * HPG + h + 1) * HEAD_DIM] for h in range(HPG)],
            axis=0).astype(BF16)

    def picked_masks(first_key, n_keys):
        key = first_key + lax.broadcasted_iota(jnp.int32, (ns_pad, n_keys), 1)
        expand = lax.broadcasted_iota(jnp.int32, (ns_pad, n_keys), 0) == key // SEL_BLOCK
        hit = _dot(sel_ref[0].astype(BF16), expand.astype(BF16))
        kpos = first_key + lax.broadcasted_iota(jnp.int32, (1, n_keys), 1)
        out = []
        for g in range(KV_HEADS):
            m1 = (hit[g * T:(g + 1) * T] > 0.5) & (kpos <= t_pos)
            out.append(jnp.concatenate([m1] * HPG, axis=0))
        return out

    def update(g, s, mask, v):
        m, l, acc = _online_step(s, mask, v, m_ref[g][:, 0:1], l_ref[g][:, 0:1], acc_ref[g])
        m_ref[g] = jnp.broadcast_to(m, (R, LANES))
        l_ref[g] = jnp.broadcast_to(l, (R, LANES))
        acc_ref[g] = acc

    masks = picked_masks(c * chunk, chunk)
    by_chunk = [jnp.swapaxes(page_refs[p][0], 0, 1) for p in range(npg)]
    for g in range(KV_HEADS):
        qs = qs_of(g)
        k = jnp.concatenate([by_chunk[p][g] for p in range(npg)], axis=0).astype(BF16)
        v = jnp.concatenate([by_chunk[p][KV_HEADS + g] for p in range(npg)], axis=0).astype(BF16)
        update(g, _dot_nt(qs, k), masks[g], v)

    @pl.when(c == pl.num_programs(1) - 1)
    def _():
        n_new = new_ref.shape[1]
        n_band = band_ref.shape[1]
        gates = gate_ref[0]
        new_masks = picked_masks(past_len, n_new)
        for g in range(KV_HEADS):
            qs = qs_of(g)
            k = new_ref[0, :, g * HEAD_DIM:(g + 1) * HEAD_DIM].astype(BF16)
            v = new_ref[0, :, KV_WIDTH + g * HEAD_DIM:KV_WIDTH + (g + 1) * HEAD_DIM].astype(BF16)
            update(g, _dot_nt(qs, k), new_masks[g], v)
            o_sel = acc_ref[g] * (1.0 / jnp.maximum(l_ref[g][:, 0:1], 1e-30))
            kpos = (past_len - WINDOW) + lax.broadcasted_iota(jnp.int32, (1, n_band), 1)
            wmask = (kpos <= t_rows) & (kpos > t_rows - WINDOW)
            kw = band_ref[0, :, g * HEAD_DIM:(g + 1) * HEAD_DIM].astype(BF16)
            vw = band_ref[0, :, KV_WIDTH + g * HEAD_DIM:KV_WIDTH + (g + 1) * HEAD_DIM].astype(BF16)
            p_win = _softmax_rows(_dot_nt(qs, kw), wmask)
            o_win = _dot(p_win.astype(BF16), vw)
            o_cmp = ocmp_ref[0, g]
            for h in range(HPG):
                rows = slice(h * T, (h + 1) * T)
                col = g * HPG + h
                o = (gates[:, col:col + 1] * o_cmp[rows]
                     + gates[:, N_HEADS + col:N_HEADS + col + 1] * o_sel[rows]
                     + gates[:, 2 * N_HEADS + col:2 * N_HEADS + col + 1] * o_win[rows])
                o_ref[0, :, col * HEAD_DIM:(col + 1) * HEAD_DIM] = o


def nsa_sample_attend(q, cmp, pages, page_ids, new_rows, band, gates, past_len):
    B, T, _ = q.shape
    npg = PAGES_PER_STEP
    ppb = page_ids.shape[0] // B
    steps = ppb // npg
    ns = (past_len + T + SEL_BLOCK - 1) // SEL_BLOCK
    ns_pad = -(-ns // LANES) * LANES
    o_cmp, sel = pl.pallas_call(
        functools.partial(_nsa_sample_select_kernel, past_len=past_len, ns_pad=ns_pad),
        grid=(B,),
        in_specs=[pl.BlockSpec((1, T, q.shape[2]), lambda b: (b, 0, 0)),
                  pl.BlockSpec((1,) + cmp.shape[1:], lambda b: (b, 0, 0))],
        out_specs=[pl.BlockSpec((1, KV_HEADS, HPG * T, HEAD_DIM), lambda b: (b, 0, 0, 0)),
                   pl.BlockSpec((1, KV_HEADS * T, ns_pad), lambda b: (b, 0, 0))],
        out_shape=[jax.ShapeDtypeStruct((B, KV_HEADS, HPG * T, HEAD_DIM), F32),
                   jax.ShapeDtypeStruct((B, KV_HEADS * T, ns_pad), F32)],
        compiler_params=_cparams(("parallel",)),
        name="nsa_sample_select",
    )(q, cmp)

    def page_spec(p):
        return pl.BlockSpec((1, PAGE_SIZE, 2 * KV_HEADS, HEAD_DIM),
                            lambda b, c, pt: (pt[b * ppb + c * npg + p], 0, 1, 0))

    per_b = lambda a: pl.BlockSpec((1,) + a.shape[1:], lambda b, c, pt: (b,) + (0,) * (a.ndim - 1))
    R = HPG * T
    return pl.pallas_call(
        functools.partial(_nsa_sample_attend_kernel, past_len=past_len, ns_pad=ns_pad),
        grid_spec=pltpu.PrefetchScalarGridSpec(
            num_scalar_prefetch=1,
            grid=(B, steps),
            in_specs=[page_spec(p) for p in range(npg)]
            + [per_b(q), per_b(sel), per_b(new_rows), per_b(band), per_b(o_cmp), per_b(gates)],
            out_specs=pl.BlockSpec((1, T, q.shape[2]), lambda b, c, pt: (b, 0, 0)),
            scratch_shapes=[pltpu.VMEM((KV_HEADS, R, LANES), F32),
                            pltpu.VMEM((KV_HEADS, R, LANES), F32),
                            pltpu.VMEM((KV_HEADS, R, HEAD_DIM), F32)]),
        out_shape=jax.ShapeDtypeStruct(q.shape, F32),
        compiler_params=_cparams(("parallel", "arbitrary")),
        name="nsa_sample_attend",
    )(page_ids, *([pages] * npg), q, sel, new_rows, band, o_cmp, gates)


def _conv_gate_kernel(b_ref, c_ref, h_ref, prev_ref, w_ref, o_ref, st_ref, carry_ref):
    i = pl.program_id(1)
    tt = b_ref.shape[0]

    @pl.when(i == 0)
    def _():
        carry_ref[...] = prev_ref[0]

    u = c_ref[...] * h_ref[...]
    row = lax.broadcasted_iota(jnp.int32, u.shape, 0)
    prev = carry_ref[...]
    u1 = jnp.where(row == 0, prev[7:8, :], pltpu.roll(u, 1, axis=0))
    u2 = jnp.where(row == 0, prev[6:7, :], jnp.where(row == 1, prev[7:8, :], pltpu.roll(u, 2, axis=0)))
    w = w_ref[...]
    conv = w[0:1, :] * u2 + w[1:2, :] * u1 + w[2:3, :] * u
    o_ref[...] = (b_ref[...] * conv).astype(o_ref.dtype)
    carry_ref[...] = u[tt - 8:, :]
    st_ref[0] = u[tt - 8:, :]


def conv_gate(proj, prev8, w_conv, B, T, *, tt, out_dtype):
    D = proj.shape[1] // 3
    nt = T // tt
    col = lambda k: pl.BlockSpec((tt, D), lambda b, i: (b * nt + i, k))
    return pl.pallas_call(
        _conv_gate_kernel,
        grid=(B, nt),
        in_specs=[col(0), col(1), col(2),
                  pl.BlockSpec((1, 8, D), lambda b, i: (b, 0, 0)),
                  pl.BlockSpec((8, D), lambda b, i: (0, 0))],
        out_specs=[pl.BlockSpec((tt, D), lambda b, i: (b * nt + i, 0)),
                   pl.BlockSpec((1, 8, D), lambda b, i: (b, 0, 0))],
        out_shape=[jax.ShapeDtypeStruct((B * T, D), out_dtype), jax.ShapeDtypeStruct((B, 8, D), F32)],
        scratch_shapes=[pltpu.VMEM((8, D), F32)],
        compiler_params=_cparams(("parallel", "arbitrary")),
        name="conv_gate",
    )(proj, proj, proj, prev8, w_conv)


def _cmlp_mix_kernel(u_ref, v_ref, g_ref, b_ref, ws_ref, bs_ref, o_ref, vo_ref):
    c = u_ref.shape[0]
    v = _ln(v_ref[...], g_ref[...], b_ref[...])
    vo_ref[...] = v
    vb = v.astype(BF16)
    dg = v.shape[1] // SG_GROUPS
    tril = lax.broadcasted_iota(jnp.int32, (c, c), 0) >= lax.broadcasted_iota(jnp.int32, (c, c), 1)
    for g in range(SG_GROUPS):
        ws = jnp.where(tril, ws_ref[g], 0.0).astype(BF16)
        mixed = _dot(ws, vb[:, g * dg:(g + 1) * dg]) + bs_ref[g]
        o_ref[:, g * dg:(g + 1) * dg] = (u_ref[:, g * dg:(g + 1) * dg] * mixed).astype(o_ref.dtype)


def cmlp_mix(proj, ln_g, ln_b, w_s, b_s, *, c, out_dtype):
    M = proj.shape[0]
    D = proj.shape[1] // 2
    const = lambda a: pl.BlockSpec(a.shape, lambda i: (0,) * a.ndim)
    ln_g = ln_g.reshape(1, D)
    ln_b = ln_b.reshape(1, D)
    return pl.pallas_call(
        _cmlp_mix_kernel,
        grid=(M // c,),
        in_specs=[pl.BlockSpec((c, D), lambda i: (i, 0)), pl.BlockSpec((c, D), lambda i: (i, 1)),
                  const(ln_g), const(ln_b), const(w_s), const(b_s)],
        out_specs=[pl.BlockSpec((c, D), lambda i: (i, 0)), pl.BlockSpec((c, D), lambda i: (i, 0))],
        out_shape=[jax.ShapeDtypeStruct((M, D), out_dtype), jax.ShapeDtypeStruct((M, D), F32)],
        compiler_params=_cparams(("parallel",)),
        name="cmlp_mix",
    )(proj, proj, ln_g, ln_b, w_s, b_s)


def _router_kernel(x_ref, w_ref, o_ref):
    logits = jnp.dot(x_ref[...], w_ref[...], preferred_element_type=F32, precision=lax.Precision.HIGHEST)
    lane = lax.broadcasted_iota(jnp.int32, logits.shape, 1)
    lane_f = lane.astype(F32)
    lg = jnp.where(lane < N_EXPERTS, logits, -jnp.inf)
    m1 = jnp.max(lg, axis=-1, keepdims=True)
    i1 = jnp.min(jnp.where(lg == m1, lane_f, float(LANES)), axis=-1, keepdims=True)
    lg2 = jnp.where(lane_f == i1, -jnp.inf, lg)
    m2 = jnp.max(lg2, axis=-1, keepdims=True)
    i2 = jnp.min(jnp.where(lg2 == m2, lane_f, float(LANES)), axis=-1, keepdims=True)
    e = jnp.exp(m2 - m1)
    den = 1.0 + e
    g1 = 1.0 / den
    g2 = e / den
    o_ref[...] = jnp.where(lane == 0, i1, jnp.where(lane == 1, i2,
                                                    jnp.where(lane == 2, g1, jnp.where(lane == 3, g2, 0.0))))


def router_top2(x, w_router, *, tm):
    M, K = x.shape
    w = jnp.pad(w_router, ((0, 0), (0, LANES - N_EXPERTS)))
    return pl.pallas_call(
        _router_kernel,
        grid=(M // tm,),
        in_specs=[pl.BlockSpec((tm, K), lambda i: (i, 0)), pl.BlockSpec((K, LANES), lambda i: (0, 0))],
        out_specs=pl.BlockSpec((tm, LANES), lambda i: (i, 0)),
        out_shape=jax.ShapeDtypeStruct((M, LANES), F32),
        compiler_params=_cparams(("parallel",)),
        name="moe_router",
    )(x, w)


def _gmm_swiglu_kernel(be_ref, nv_ref, new_ref, x_ref, wg_ref, wu_ref, o_ref, wgb_ref, wub_ref):
    r = pl.program_id(1)

    @pl.when(r < nv_ref[0])
    def _():
        @pl.when(new_ref[r] == 1)
        def _():
            wgb_ref[...] = wg_ref[0, 0].astype(BF16)
            wub_ref[...] = wu_ref[0, 0].astype(BF16)

        x = x_ref[...]
        g = _dot(x, wgb_ref[...])
        u = _dot(x, wub_ref[...])
        o_ref[...] = (jax.nn.silu(g) * u).astype(o_ref.dtype)


def gmm_swiglu(xs, w_gu, block_e, n_valid, new_expert, *, rb, tn, layer):
    P, K = xs.shape
    F = w_gu.shape[3] // 2
    nf = F // tn
    row = lambda r, nv: jnp.minimum(r, nv[0] - 1)
    return pl.pallas_call(
        _gmm_swiglu_kernel,
        grid_spec=pltpu.PrefetchScalarGridSpec(
            num_scalar_prefetch=3,
            grid=(nf, P // rb),
            in_specs=[pl.BlockSpec((rb, K), lambda n, r, be, nv, nw: (row(r, nv), 0)),
                      pl.BlockSpec((1, 1, K, tn), lambda n, r, be, nv, nw: (layer, be[row(r, nv)], 0, n)),
                      pl.BlockSpec((1, 1, K, tn), lambda n, r, be, nv, nw: (layer, be[row(r, nv)], 0, n + nf))],
            out_specs=pl.BlockSpec((rb, tn), lambda n, r, be, nv, nw: (row(r, nv), n)),
            scratch_shapes=[pltpu.VMEM((K, tn), BF16), pltpu.VMEM((K, tn), BF16)]),
        out_shape=jax.ShapeDtypeStruct((P, F), BF16),
        compiler_params=_cparams(("arbitrary", "arbitrary")),
        name="moe_gmm_swiglu",
    )(block_e, n_valid, new_expert, xs, w_gu, w_gu)


_IT_FIRST, _IT_LAST, _IT_NEW_W, _IT_SKIP = 1, 2, 4, 8


def _gmm_down_kernel(blk_ref, k_ref, e_ref, slot_ref, oblk_ref, flag_ref, h_ref, w_ref, o_ref, wb_ref, acc_ref):
    w = pl.program_id(0)
    f = flag_ref[w]

    @pl.when((f & _IT_SKIP) == 0)
    def _():
        @pl.when((f & _IT_NEW_W) != 0)
        def _():
            wb_ref[...] = w_ref[0, 0].astype(BF16)

        slot = slot_ref[w]
        part = _dot(h_ref[...], wb_ref[...])

        @pl.when((f & _IT_FIRST) != 0)
        def _():
            acc_ref[slot] = part

        @pl.when((f & _IT_FIRST) == 0)
        def _():
            acc_ref[slot] += part

        @pl.when((f & _IT_LAST) != 0)
        def _():
            o_ref[...] = acc_ref[slot]


def gmm_down(h, w_down, items, *, rb, tk, run, layer):
    P, F = h.shape
    D = w_down.shape[3]
    it_blk, it_k, it_e, it_slot, it_oblk, it_flag = items
    n_items = it_blk.shape[0]
    return pl.pallas_call(
        _gmm_down_kernel,
        grid_spec=pltpu.PrefetchScalarGridSpec(
            num_scalar_prefetch=6,
            grid=(n_items,),
            in_specs=[pl.BlockSpec((rb, tk), lambda w, blk, k, e, s, ob, fl: (blk[w], k[w])),
                      pl.BlockSpec((1, 1, tk, D), lambda w, blk, k, e, s, ob, fl: (layer, e[w], k[w], 0))],
            out_specs=pl.BlockSpec((rb, D), lambda w, blk, k, e, s, ob, fl: (ob[w], 0)),
            scratch_shapes=[pltpu.VMEM((tk, D), BF16), pltpu.VMEM((run, rb, D), F32)]),
        out_shape=jax.ShapeDtypeStruct((P, D), F32),
        compiler_params=_cparams(("arbitrary",)),
        name="moe_gmm_down",
    )(it_blk, it_k, it_e, it_slot, it_oblk, it_flag, h, w_down)


def _moe_combine_ln_kernel(res_ref, y0_ref, y1_ref, r_ref, g_ref, b_ref, o_ref, ob_ref):
    r = r_ref[...]
    f = r[:, 2:3] * y0_ref[...] + r[:, 3:4] * y1_ref[...]
    y = _ln(DEEPNORM_ALPHA * res_ref[...] + f, g_ref[...], b_ref[...])
    o_ref[...] = y
    ob_ref[...] = y.astype(BF16)


def moe_combine_ln(res, y0, y1, routed, gain, bias, *, tm):
    M, N = res.shape
    row = pl.BlockSpec((tm, N), lambda i: (i, 0))
    vec = pl.BlockSpec((1, N), lambda i: (0, 0))
    return pl.pallas_call(
        _moe_combine_ln_kernel,
        grid=(M // tm,),
        in_specs=[row, row, row, pl.BlockSpec((tm, LANES), lambda i: (i, 0)), vec, vec],
        out_specs=[row, row],
        out_shape=[jax.ShapeDtypeStruct((M, N), F32), jax.ShapeDtypeStruct((M, N), BF16)],
        compiler_params=_cparams(("parallel",)),
        name="moe_combine_ln",
    )(res, y0, y1, routed, gain.reshape(1, N), bias.reshape(1, N))


def moe_dispatch(top_e, *, rb, run, nk):
    A = top_e.size
    flat_e = top_e.reshape(A)
    experts = jnp.arange(N_EXPERTS, dtype=jnp.int32)
    onehot = (flat_e[:, None] == experts[None, :]).astype(jnp.int32)
    pos_in_e = jnp.sum((jnp.cumsum(onehot, axis=0) - onehot) * onehot, axis=1)
    counts = jnp.sum(onehot, axis=0)
    nblk_e = (counts + rb - 1) // rb
    bend_e = jnp.cumsum(nblk_e)
    bstart_e = bend_e - nblk_e
    dest = (bstart_e[flat_e] * rb + pos_in_e).astype(jnp.int32)
    n_blocks = -(-(A + N_EXPERTS * (rb - 1)) // rb)
    row_tok = jnp.zeros((n_blocks * rb,), jnp.int32).at[dest].set(jnp.arange(A, dtype=jnp.int32) // TOP_K)
    b = jnp.arange(n_blocks, dtype=jnp.int32)
    block_e = jnp.minimum(jnp.sum((bend_e[None, :] <= b[:, None]).astype(jnp.int32), axis=1), N_EXPERTS - 1)
    n_valid = bend_e[-1]
    new_expert = ((b == 0) | (block_e != jnp.roll(block_e, 1))).astype(jnp.int32)
    idx_in_e = b - bstart_e[block_e]
    slot = idx_in_e % run
    first = b - slot
    n_run = jnp.minimum(run, nblk_e[block_e] - (idx_in_e - slot))
    k = jnp.arange(nk, dtype=jnp.int32)[None, :]
    n_items = nk * n_blocks
    w_idx = nk * first[:, None] + k * n_run[:, None] + slot[:, None]
    w_idx = jnp.where((b < n_valid)[:, None], w_idx, n_items).reshape(-1)
    flag = (jnp.where(k == 0, _IT_FIRST, 0) | jnp.where(k == nk - 1, _IT_LAST, 0)
            | jnp.where(slot[:, None] == 0, _IT_NEW_W, 0))
    full = lambda v: jnp.broadcast_to(v, (n_blocks, nk)).reshape(-1).astype(jnp.int32)
    cols = [full(b[:, None]), full(k), full(block_e[:, None]), full(slot[:, None]),
            full(jnp.where(k == nk - 1, b[:, None], first[:, None])), full(flag)]
    items = [jnp.zeros((n_items,), jnp.int32).at[w_idx].set(c, mode="drop") for c in cols]
    w = jnp.arange(n_items, dtype=jnp.int32)
    last = nk * n_valid - 1
    items = [jnp.where(w <= last, c, c[last]) for c in items]
    items[5] = jnp.where(w <= last, items[5], _IT_SKIP)
    return dest, row_tok, block_e, n_valid.reshape(1), new_expert, items


MOE_RB = 384
MOE_RUN = 6
MOE_TN = 1024
MOE_TK = 1024


def moe_layer(xp, xpb, xs, xsb, w_router, w_gu, w_down, gain, bias, *, layer):
    Mp, Ms = xp.shape[0], xs.shape[0]
    rp = router_top2(xp, w_router, tm=1024)
    rs = router_top2(xs, w_router, tm=Ms)
    top_e = jnp.concatenate([rp[:, 0:2], rs[:, 0:2]], axis=0).astype(jnp.int32)
    nk = w_down.shape[2] // MOE_TK
    dest, row_tok, block_e, n_valid, new_expert, items = moe_dispatch(top_e, rb=MOE_RB, run=MOE_RUN, nk=nk)
    x_rows = jnp.concatenate([xpb, xsb], axis=0)[row_tok]
    h = gmm_swiglu(x_rows, w_gu, block_e, n_valid, new_expert, rb=MOE_RB, tn=MOE_TN, layer=layer)
    y = gmm_down(h, w_down, items, rb=MOE_RB, tk=MOE_TK, run=MOE_RUN, layer=layer)
    dest = dest.reshape(Mp + Ms, TOP_K)
    dp, ds = dest[:Mp], dest[Mp:]
    xp, xpb = moe_combine_ln(xp, y[dp[:, 0]], y[dp[:, 1]], rp, gain, bias, tm=512)
    xs, xsb = moe_combine_ln(xs, y[ds[:, 0]], y[ds[:, 1]], rs, gain, bias, tm=Ms)
    return xp, xpb, xs, xsb


def _compress_weights(cmp_w1, cmp_pe, cmp_w2):
    s = CMP_STRIDE
    wcat = jnp.concatenate([cmp_w1[0, :s], cmp_w1[0, s:], cmp_w1[1, :s], cmp_w1[1, s:]], axis=-1)
    wcat = wcat.reshape(s // 2, 2 * HEAD_DIM, 4 * HEAD_DIM).astype(BF16)
    return wcat, compress_pe_term(cmp_w1, cmp_pe), cmp_w2.astype(BF16)


def _gates_by_group(gates, M):
    g = gates[:, :3 * N_HEADS].reshape(M, 3, KV_HEADS, HPG).transpose(2, 0, 1, 3).reshape(KV_HEADS, M, 3 * HPG)
    return jnp.pad(g, ((0, 0), (0, 0), (0, LANES - 3 * HPG)))


def _gate_weights(w_in, layer):
    c0 = N_HEADS * HEAD_DIM + 6 * KV_WIDTH
    return jnp.pad(_stacked(w_in)[layer, :, c0:], ((0, 0), (0, LANES - 3 * N_HEADS)))


def _sub_block_view(rows):
    return rows.reshape(-1, PAGE_SIZE // CMP_STRIDE, CMP_STRIDE, 4 * KV_HEADS, HEAD_DIM)


def nsa_prompt_mixer(xb, w_in, cmp_w, cos, sin, B, T, *, tm, layer=0):
    M = B * T
    wcat, pe, w2 = cmp_w
    q, rows, win = nsa_proj(xb, w_in, cos, sin, tm=tm, q_dtype=BF16, layer=layer)
    gates = linear(xb, _gate_weights(w_in, layer), tm=tm, tn=LANES, act="sigmoid")
    page_ids = jnp.arange(M // PAGE_SIZE, dtype=jnp.int32)
    cmp = compress(_sub_block_view(rows), page_ids, B, wcat, pe, w2)
    o = nsa_prompt_attend(q, cmp, rows.reshape(B, T, 4 * KV_WIDTH), win.reshape(B, T, 2 * KV_WIDTH),
                          _gates_by_group(gates, M), B, T)
    wn = min(WINDOW, T)
    return (o, rows.reshape(B, T, 4, KV_HEADS, HEAD_DIM),
            win.reshape(B, T, 2, KV_HEADS, HEAD_DIM)[:, T - wn:])


def nsa_sample_mixer(xb, w_in, cmp_w, cos, sin, cache, page_ids, win_buf, B, T, *, layer=0):
    M = B * T
    past_len = page_ids.shape[0] // B * PAGE_SIZE
    wcat, pe, w2 = cmp_w
    q, rows, win = nsa_proj(xb, w_in, cos, sin, tm=M, q_dtype=F32, layer=layer)
    gates = linear(xb, _gate_weights(w_in, layer), tm=M, tn=LANES, act="sigmoid")
    cmp = compress(_sub_block_view(cache), page_ids, B, wcat, pe, w2)
    pages = cache.reshape(-1, PAGE_SIZE, 4 * KV_HEADS, HEAD_DIM)
    rows3 = rows.reshape(B, T, 4 * KV_WIDTH)
    new_rows = jnp.pad(rows3[:, :, 2 * KV_WIDTH:], ((0, 0), (0, LANES - T), (0, 0)))
    nbuf = win_buf.shape[1]
    band = jnp.concatenate([win_buf.reshape(B, nbuf, 2 * KV_WIDTH), win.reshape(B, T, 2 * KV_WIDTH)], axis=1)
    band_pad = jnp.pad(band, ((0, 0), (0, WINDOW + LANES - band.shape[1]), (0, 0)))
    o = nsa_sample_attend(q.reshape(B, T, -1), cmp, pages, page_ids, new_rows, band_pad,
                          gates.reshape(B, T, LANES), past_len)
    return (o.reshape(M, -1), rows.reshape(B, T, 4, KV_HEADS, HEAD_DIM),
            band[:, band.shape[1] - nbuf:].reshape(B, nbuf, 2, KV_HEADS, HEAD_DIM))


def conv_mixer(xb, prev, w_in, w_conv, B, T, *, tm, tt, layer=0):
    keep = CONV_WIDTH - 1
    proj = linear(xb, w_in, tm=tm, tn=512, layer=layer)
    prev8 = jnp.pad(prev, ((0, 0), (8 - keep, 0), (0, 0)))
    w8 = jnp.pad(w_conv, ((0, 8 - CONV_WIDTH), (0, 0)))
    gated, st = conv_gate(proj, prev8, w8, B, T, tt=tt, out_dtype=BF16 if tt % 16 == 0 else F32)
    return gated, st[:, 8 - keep:]


def chunk_mlp_mixer(xb, w_in, ln_g, ln_b, w_s, b_s, B, T, *, tm, layer=0):
    D = w_in.shape[-1] // 2
    c = min(T, CHUNK)
    proj = linear(xb, w_in, tm=tm, tn=512, act="gelu", layer=layer)
    cp = max(c, 16)
    if cp != c:
        proj = jnp.pad(proj.reshape(B * T // c, c, 2 * D), ((0, 0), (0, cp - c), (0, 0))).reshape(-1, 2 * D)
    ws = jnp.pad(w_s[:, :c, :c], ((0, 0), (0, cp - c), (0, cp - c)))
    bs = jnp.pad(b_s[:, :c, None], ((0, 0), (0, cp - c), (0, 0)))
    mixed, v = cmlp_mix(proj, ln_g, ln_b, ws, bs, c=cp, out_dtype=BF16 if c == cp else F32)
    if cp != c:
        mixed = mixed.reshape(-1, cp, D)[:, :c].reshape(B * T, D)
        v = v.reshape(-1, cp, D)[:, :c].reshape(B * T, D)
    return mixed, v.reshape(B, T, D)[:, T - c:]


def kernel(x_prompt, x_sample, cache_nsa_kv, state_nsa_win, state_conv, page_table, ln_gain, ln_bias, nsa_w_in, nsa_cmp_w1, nsa_cmp_pe, nsa_cmp_w2, nsa_w_out, conv_w_in, conv_w, conv_w_out, cmlp_w_in, cmlp_ln_gain, cmlp_ln_bias, cmlp_w_s, cmlp_b_s, cmlp_w_out, ffn_w_gu, ffn_w_down, moe_router, moe_w_gu, moe_w_down):
    B, T, D = x_prompt.shape
    SB, ST, _ = x_sample.shape
    Mp, Ms = B * T, SB * ST
    n_pool = cache_nsa_kv.shape[1]
    n_pages = page_table.shape[1]
    past_len = n_pages * PAGE_SIZE
    tm_p, tm_s = 1024, Ms

    xp = x_prompt.reshape(Mp, D)
    xs = x_sample.reshape(Ms, D)
    xpb, xsb = xp, xs

    pos_p = jnp.tile(jnp.arange(T, dtype=jnp.int32), B)
    pos_s = jnp.tile(past_len + jnp.arange(ST, dtype=jnp.int32), SB)
    cos_p, sin_p = rope_tables(pos_p)
    cos_s, sin_s = rope_tables(pos_s)
    nsa_w_out_b = nsa_w_out.astype(BF16)
    conv_w_out_b = conv_w_out.astype(BF16)
    cmlp_w_out_b = cmlp_w_out.astype(BF16)

    rows_p, rows_s, win_p, win_s = [], [], [], []
    conv_p, conv_s, v_p, v_s = [], [], [], []
    for i in range(DEPTH):
        m = i // 3
        kind = i % 3
        g0, b0 = ln_gain[i, 0], ln_bias[i, 0]
        if kind == 0:
            w = (nsa_w_in, _compress_weights(nsa_cmp_w1[m], nsa_cmp_pe[m], nsa_cmp_w2[m]))
            o, rows, win = nsa_prompt_mixer(xpb, *w, cos_p, sin_p, B, T, tm=tm_p, layer=m)
            xp, xpb = linear_res_ln(o, nsa_w_out_b, xp, g0, b0, tm=512, tk=D, layer=m)
            rows_p.append(rows)
            win_p.append(win)
            page_ids = (m * n_pool + page_table).reshape(-1).astype(jnp.int32)
            o, rows, win = nsa_sample_mixer(xsb, *w, cos_s, sin_s, cache_nsa_kv, page_ids, state_nsa_win[m],
                                            SB, ST, layer=m)
            xs, xsb = linear_res_ln(o, nsa_w_out_b, xs, g0, b0, tm=tm_s, tk=D, layer=m)
            rows_s.append(rows)
            win_s.append(win)
        elif kind == 1:
            w = (conv_w_in, conv_w[m])
            gated, st = conv_mixer(xpb, jnp.zeros((B, CONV_WIDTH - 1, D), F32), *w, B, T, tm=tm_p, tt=256,
                                   layer=m)
            xp, xpb = linear_res_ln(gated, conv_w_out_b, xp, g0, b0, tm=512, tk=D, layer=m)
            conv_p.append(st)
            gated, st = conv_mixer(xsb, state_conv[m], *w, SB, ST, tm=tm_s, tt=ST, layer=m)
            xs, xsb = linear_res_ln(gated, conv_w_out_b, xs, g0, b0, tm=tm_s, tk=D, layer=m)
            conv_s.append(st)
        else:
            w = (cmlp_w_in, cmlp_ln_gain[m], cmlp_ln_bias[m], cmlp_w_s[m], cmlp_b_s[m])
            mixed, v = chunk_mlp_mixer(xpb, *w, B, T, tm=tm_p, layer=m)
            xp, xpb = linear_res_ln(mixed, cmlp_w_out_b, xp, g0, b0, tm=512, tk=D, layer=m)
            v_p.append(v)
            mixed, v = chunk_mlp_mixer(xsb, *w, SB, ST, tm=tm_s, layer=m)
            xs, xsb = linear_res_ln(mixed, cmlp_w_out_b, xs, g0, b0, tm=tm_s, tk=D, layer=m)
            v_s.append(v)
        g1, b1 = ln_gain[i, 1], ln_bias[i, 1]
        f = i // 2
        if i % 2 == 0:
            h = linear_swiglu(xpb, ffn_w_gu, tm=tm_p, tn=512, layer=f)
            xp, xpb = linear_res_ln(h, ffn_w_down, xp, g1, b1, tm=1024, tk=512, layer=f)
            h = linear_swiglu(xsb, ffn_w_gu, tm=tm_s, tn=512, layer=f)
            xs, xsb = linear_res_ln(h, ffn_w_down, xs, g1, b1, tm=tm_s, tk=512, layer=f)
        else:
            xp, xpb, xs, xsb = moe_layer(xp, xpb, xs, xsb, moe_router[f], moe_w_gu, moe_w_down, g1, b1,
                                         layer=f)
    return (xp.reshape(B, T, D), xs.reshape(SB, ST, D),
            jnp.stack(rows_p), jnp.stack(rows_s), jnp.stack(win_p), jnp.stack(win_s),
            jnp.stack(conv_p), jnp.stack(conv_s), jnp.stack(v_p), jnp.stack(v_s))
```
